```python
import math, functools
import jax, jax.numpy as jnp
from jax import lax
import numpy as np

D_MODEL = 1024
BATCH = 4
SEQ = 8192
DEPTH = 2
DEC_BATCH = 16
DEC_SEQ = 2048
PAST_LEN = 128

HEAD_DIM = 64
ATTN_WIDTH = D_MODEL // 2
N_Q_HEADS = ATTN_WIDTH // HEAD_DIM
N_KV_HEADS = 2
KV_WIDTH = N_KV_HEADS * HEAD_DIM
WINDOW = 128
BLOCK = 128
ROPE_THETA = 10000.0
POOL_WINDOWS = (2, 4, 8, 16)
N_POOL_GROUPS = len(POOL_WINDOWS)
POOL_WIDTH = D_MODEL // 2
POOL_GROUP_DIM = POOL_WIDTH // N_POOL_GROUPS
IN_PROJ_WIDTH = ATTN_WIDTH + 2 * KV_WIDTH + POOL_WIDTH
MIX_WIDTH = ATTN_WIDTH + POOL_WIDTH
CONV_DIM = D_MODEL
CONV_WIDTH = 31
D_FF = -(-8 * D_MODEL // (3 * 256)) * 256
N_EVEN = (DEPTH + 1) // 2
N_ODD = DEPTH // 2
EPS = 1e-6
NEG_INF = -1e30

kernel_name = 'hybrid_window_gqa_pool_conformer_encoder'


def _rmsnorm(x, g):
    xf = x.astype(jnp.float32)
    y = xf * lax.rsqrt(jnp.mean(xf * xf, axis=-1, keepdims=True) + EPS)
    return (y * g.astype(jnp.float32)).astype(x.dtype)


def _rope(x, pos):
    half = HEAD_DIM // 2
    inv_freq = ROPE_THETA ** (-jnp.arange(0, half, dtype=jnp.float32) * 2.0 / HEAD_DIM)
    ang = pos.astype(jnp.float32)[:, None] * inv_freq[None, :]
    cos = jnp.cos(ang)[None, :, None, :]
    sin = jnp.sin(ang)[None, :, None, :]
    xf = x.astype(jnp.float32)
    x1, x2 = xf[..., :half], xf[..., half:]
    out = jnp.concatenate([x1 * cos - x2 * sin, x2 * cos + x1 * sin], axis=-1)
    return out.astype(x.dtype)


def _windowed_gqa(q, k, v, sink):
    B, S = q.shape[0], q.shape[1]
    nb = S // BLOCK
    G = N_Q_HEADS // N_KV_HEADS
    qb = q.reshape(B, nb, BLOCK, N_KV_HEADS, G, HEAD_DIM)

    def band(t):
        tp = jnp.pad(t, ((0, 0), (BLOCK, BLOCK), (0, 0), (0, 0)))
        tp = tp.reshape(B, nb + 2, BLOCK, N_KV_HEADS, HEAD_DIM)
        return jnp.concatenate([tp[:, :-2], tp[:, 1:-1], tp[:, 2:]], axis=2)

    kb, vb = band(k), band(v)
    scores = jnp.einsum('bnqhgd,bnjhd->bnhgqj', qb, kb,
                        preferred_element_type=jnp.float32) * (HEAD_DIM ** -0.5)
    blk = jnp.arange(nb)[:, None, None]
    qpos = blk * BLOCK + jnp.arange(BLOCK)[None, :, None]
    kpos = blk * BLOCK - BLOCK + jnp.arange(3 * BLOCK)[None, None, :]
    mask = (jnp.abs(kpos - qpos) <= WINDOW) & (kpos >= 0) & (kpos < S)
    scores = jnp.where(mask[None, :, None, None], scores, NEG_INF)
    sink_l = sink.astype(jnp.float32).reshape(N_KV_HEADS, G)[None, None, :, :, None, None]
    m = jnp.maximum(jnp.max(scores, axis=-1, keepdims=True), sink_l)
    p = jnp.exp(scores - m)
    p = p / (jnp.sum(p, axis=-1, keepdims=True) + jnp.exp(sink_l - m))
    out = jnp.einsum('bnhgqj,bnjhd->bnqhgd', p.astype(v.dtype), vb)
    return out.reshape(B, S, N_Q_HEADS * HEAD_DIM)


def _multiscale_pool(u, w_pool, pool_scale):
    B, S = u.shape[0], u.shape[1]
    pos = jnp.arange(S)
    outs = []
    for gi, w in enumerate(POOL_WINDOWS):
        half = w // 2
        ug = u[..., gi * POOL_GROUP_DIM:(gi + 1) * POOL_GROUP_DIM].astype(jnp.float32)
        up = jnp.pad(ug, ((0, 0), (half, half), (0, 0)))
        cs = jnp.pad(jnp.cumsum(up, axis=1), ((0, 0), (1, 0), (0, 0)))
        wsum = cs[:, w:w + S] - cs[:, :S]
        cnt = (jnp.minimum(pos + half, S) - jnp.maximum(pos - half, 0)).astype(jnp.float32)
        outs.append(wsum / cnt[None, :, None] - ug)
    pooled = jnp.stack(outs, axis=2)
    mixed = jnp.einsum('bsgc,gcd->bsgd', pooled, w_pool.astype(jnp.float32))
    mixed = mixed.reshape(B, S, POOL_WIDTH) * pool_scale.astype(jnp.float32)
    return mixed.astype(u.dtype)


def _attn_pool_mixer(h, w_in, sink, w_pool, pool_scale, w_out):
    B, S, _ = h.shape
    z = h @ w_in
    q = z[..., :ATTN_WIDTH].reshape(B, S, N_Q_HEADS, HEAD_DIM)
    k = z[..., ATTN_WIDTH:ATTN_WIDTH + KV_WIDTH].reshape(B, S, N_KV_HEADS, HEAD_DIM)
    v = z[..., ATTN_WIDTH + KV_WIDTH:ATTN_WIDTH + 2 * KV_WIDTH].reshape(B, S, N_KV_HEADS, HEAD_DIM)
    u = z[..., ATTN_WIDTH + 2 * KV_WIDTH:]
    pos = jnp.arange(S)
    a = _windowed_gqa(_rope(q, pos), _rope(k, pos), v, sink)
    p = _multiscale_pool(u, w_pool, pool_scale)
    return jnp.concatenate([a, p], axis=-1) @ w_out


def _conformer_conv(h, w_pw1, b_pw1, w_dw, b_dw, ln_g, ln_b, w_pw2, b_pw2):
    z = h @ w_pw1 + b_pw1
    g = z[..., :CONV_DIM] * jax.nn.sigmoid(z[..., CONV_DIM:])
    y = lax.conv_general_dilated(
        g, w_dw[:, None, :].astype(g.dtype), window_strides=(1,),
        padding=((CONV_WIDTH // 2, CONV_WIDTH // 2),),
        dimension_numbers=('NWC', 'WIO', 'NWC'),
        feature_group_count=CONV_DIM) + b_dw
    yf = y.astype(jnp.float32)
    mu = jnp.mean(yf, axis=-1, keepdims=True)
    var = jnp.mean(jnp.square(yf - mu), axis=-1, keepdims=True)
    yf = (yf - mu) * lax.rsqrt(var + EPS) * ln_g.astype(jnp.float32) + ln_b.astype(jnp.float32)
    y = jax.nn.silu(yf).astype(h.dtype)
    return y @ w_pw2 + b_pw2


def _swiglu(h, w_gate, w_up, w_down):
    return (jax.nn.silu(h @ w_gate) * (h @ w_up)) @ w_down


def _trunk(x, mix_pre_g, mix_post_g, ffn_pre_g, ffn_post_g,
           w_in, attn_sink, w_pool, pool_scale, w_out,
           conv_w_pw1, conv_b_pw1, conv_w_dw, conv_b_dw, conv_ln_g, conv_ln_b,
           conv_w_pw2, conv_b_pw2, ffn_w_gate, ffn_w_up, ffn_w_down):
    for layer in range(DEPTH):
        h = _rmsnorm(x, mix_pre_g[layer])
        if layer % 2 == 0:
            e = layer // 2
            m = _attn_pool_mixer(h, w_in[e], attn_sink[e], w_pool[e], pool_scale[e], w_out[e])
        else:
            o = layer // 2
            m = _conformer_conv(h, conv_w_pw1[o], conv_b_pw1[o], conv_w_dw[o], conv_b_dw[o],
                                conv_ln_g[o], conv_ln_b[o], conv_w_pw2[o], conv_b_pw2[o])
        x = x + _rmsnorm(m, mix_post_g[layer])
        h = _rmsnorm(x, ffn_pre_g[layer])
        f = _swiglu(h, ffn_w_gate[layer], ffn_w_up[layer], ffn_w_down[layer])
        x = x + _rmsnorm(f, ffn_post_g[layer])
    return x


def setup_inputs(seed: int = 0) -> dict:
    key = jax.random.key(seed)
    ks = jax.random.split(key, 24)
    f32 = jnp.float32
    nrm = lambda k, shape, s: jax.random.normal(k, shape, f32) * s
    gain = lambda k, shape: 1.0 + 0.05 * jax.random.normal(k, shape, f32)
    return {
        'x_prompt': jax.random.normal(ks[0], (BATCH, SEQ, D_MODEL), f32),
        'x_sample': jax.random.normal(ks[1], (DEC_BATCH, DEC_SEQ, D_MODEL), f32),
        'mix_pre_g': gain(ks[2], (DEPTH, D_MODEL)),
        'mix_post_g': gain(ks[3], (DEPTH, D_MODEL)),
        'ffn_pre_g': gain(ks[4], (DEPTH, D_MODEL)),
        'ffn_post_g': gain(ks[5], (DEPTH, D_MODEL)),
        'w_in': nrm(ks[6], (N_EVEN, D_MODEL, IN_PROJ_WIDTH), D_MODEL ** -0.5),
        'attn_sink': nrm(ks[7], (N_EVEN, N_Q_HEADS), 0.5),
        'w_pool': nrm(ks[8], (N_EVEN, N_POOL_GROUPS, POOL_GROUP_DIM, POOL_GROUP_DIM), POOL_GROUP_DIM ** -0.5),
        'pool_scale': gain(ks[9], (N_EVEN, POOL_WIDTH)),
        'w_out': nrm(ks[10], (N_EVEN, MIX_WIDTH, D_MODEL), MIX_WIDTH ** -0.5),
        'conv_w_pw1': nrm(ks[11], (N_ODD, D_MODEL, 2 * CONV_DIM), D_MODEL ** -0.5),
        'conv_b_pw1': nrm(ks[12], (N_ODD, 2 * CONV_DIM), 0.02),
        'conv_w_dw': nrm(ks[13], (N_ODD, CONV_WIDTH, CONV_DIM), CONV_WIDTH ** -0.5),
        'conv_b_dw': nrm(ks[14], (N_ODD, CONV_DIM), 0.02),
        'conv_ln_g': gain(ks[15], (N_ODD, CONV_DIM)),
        'conv_ln_b': nrm(ks[16], (N_ODD, CONV_DIM), 0.02),
        'conv_w_pw2': nrm(ks[17], (N_ODD, CONV_DIM, D_MODEL), CONV_DIM ** -0.5),
        'conv_b_pw2': nrm(ks[18], (N_ODD, D_MODEL), 0.02),
        'ffn_w_gate': nrm(ks[19], (DEPTH, D_MODEL, D_FF), D_MODEL ** -0.5),
        'ffn_w_up': nrm(ks[20], (DEPTH, D_MODEL, D_FF), D_MODEL ** -0.5),
        'ffn_w_down': nrm(ks[21], (DEPTH, D_FF, D_MODEL), D_FF ** -0.5),
    }


def reference(x_prompt, x_sample, mix_pre_g, mix_post_g, ffn_pre_g, ffn_post_g,
              w_in, attn_sink, w_pool, pool_scale, w_out,
              conv_w_pw1, conv_b_pw1, conv_w_dw, conv_b_dw, conv_ln_g, conv_ln_b,
              conv_w_pw2, conv_b_pw2, ffn_w_gate, ffn_w_up, ffn_w_down):
    trunk = functools.partial(
        _trunk, mix_pre_g=mix_pre_g, mix_post_g=mix_post_g, ffn_pre_g=ffn_pre_g,
        ffn_post_g=ffn_post_g, w_in=w_in, attn_sink=attn_sink, w_pool=w_pool,
        pool_scale=pool_scale, w_out=w_out, conv_w_pw1=conv_w_pw1, conv_b_pw1=conv_b_pw1,
        conv_w_dw=conv_w_dw, conv_b_dw=conv_b_dw, conv_ln_g=conv_ln_g, conv_ln_b=conv_ln_b,
        conv_w_pw2=conv_w_pw2, conv_b_pw2=conv_b_pw2, ffn_w_gate=ffn_w_gate,
        ffn_w_up=ffn_w_up, ffn_w_down=ffn_w_down)
    y_prompt = trunk(x_prompt)
    y_sample = trunk(x_sample)
    return (y_prompt, y_sample)
```

```python
import functools

import jax
import jax.numpy as jnp
from jax import lax
from jax.experimental import pallas as pl
from jax.experimental.pallas import tpu as pltpu

D_MODEL = 1024
HEAD_DIM = 64
N_Q_HEADS = 8
N_KV_HEADS = 2
ATTN_WIDTH = N_Q_HEADS * HEAD_DIM
KV_WIDTH = N_KV_HEADS * HEAD_DIM
WINDOW = 128
BLOCK = 128
BAND = 3 * BLOCK
ROPE_THETA = 10000.0
POOL_WINDOWS = (2, 4, 8, 16)
POOL_GROUP_DIM = 128
POOL_WIDTH = 512
POOL_HALO = 8
IN_PROJ_WIDTH = ATTN_WIDTH + 2 * KV_WIDTH + POOL_WIDTH
CONV_WIDTH = 31
CONV_HALO = 16
D_FF = 2816
EPS = 1e-6
NEG_INF = -1e30

LANES = 128
N_CHUNKS = D_MODEL // LANES
TOKEN_TILE = 512
FF_CHUNK = 256
CONV_ROWS = 64
VMEM_LIMIT_BYTES = 56 * 1024 * 1024

F32 = jnp.float32
BF16 = jnp.bfloat16


def _rmsnorm(x, g):
    ms = jnp.mean(x * x, axis=-1, keepdims=True)
    return (x * lax.rsqrt(ms + EPS)) * g


def _params(n_axes):
    return pltpu.CompilerParams(
        dimension_semantics=("parallel",) * n_axes, vmem_limit_bytes=VMEM_LIMIT_BYTES)


def _const_spec(shape):
    zeros = (0,) * len(shape)
    return pl.BlockSpec(shape, lambda *_: zeros, pipeline_mode=pl.Buffered(1))


def _inproj_kernel(x_ref, g_ref, w_ref, cos_ref, sin_ref, q_ref, k4_ref, v4_ref, u_ref):
    t = x_ref.shape[1]
    h = _rmsnorm(x_ref[0], g_ref[...])
    z = jnp.dot(h.astype(BF16), w_ref[...], preferred_element_type=F32)
    cos = cos_ref[...]
    sin = sin_ref[...]
    lane = lax.broadcasted_iota(jnp.int32, (t, LANES), 1)
    first_half = (lane & (HEAD_DIM - 1)) < HEAD_DIM // 2
    low = lane < HEAD_DIM

    def rope(zc):
        swapped = jnp.where(first_half, pltpu.roll(zc, LANES - HEAD_DIM // 2, 1),
                            pltpu.roll(zc, HEAD_DIM // 2, 1))
        return zc * cos + swapped * sin

    def pad_heads(a):
        ar = pltpu.roll(a, HEAD_DIM, 1)
        zero = jnp.zeros_like(a)
        return (jnp.where(low, a, zero), jnp.where(low, zero, ar),
                jnp.where(low, ar, zero), jnp.where(low, zero, a))

    for c in range(ATTN_WIDTH // LANES):
        q_ref[0, :, c * LANES:(c + 1) * LANES] = rope(z[:, c * LANES:(c + 1) * LANES]).astype(BF16)
    k = rope(z[:, ATTN_WIDTH:ATTN_WIDTH + KV_WIDTH])
    v = z[:, ATTN_WIDTH + KV_WIDTH:ATTN_WIDTH + 2 * KV_WIDTH]
    for j, (kp, vp) in enumerate(zip(pad_heads(k), pad_heads(v))):
        k4_ref[0, :, j * LANES:(j + 1) * LANES] = kp.astype(BF16)
        v4_ref[0, :, j * LANES:(j + 1) * LANES] = vp.astype(BF16)
    u_ref[0] = z[:, ATTN_WIDTH + 2 * KV_WIDTH:]


def _inproj(x, g, w_in_b, cos, sin):
    b, s, _ = x.shape
    t = TOKEN_TILE
    tok = lambda width: pl.BlockSpec((1, t, width), lambda bi, i: (bi, i, 0))
    table = pl.BlockSpec((t, LANES), lambda bi, i: (i, 0))
    return pl.pallas_call(
        _inproj_kernel,
        grid=(b, s // t),
        in_specs=[tok(D_MODEL), _const_spec((1, D_MODEL)), _const_spec((D_MODEL, IN_PROJ_WIDTH)),
                  table, table],
        out_specs=[tok(ATTN_WIDTH), tok(4 * LANES), tok(4 * LANES), tok(POOL_WIDTH)],
        out_shape=[jax.ShapeDtypeStruct((b, s, ATTN_WIDTH), BF16),
                   jax.ShapeDtypeStruct((b, s, 4 * LANES), BF16),
                   jax.ShapeDtypeStruct((b, s, 4 * LANES), BF16),
                   jax.ShapeDtypeStruct((b, s, POOL_WIDTH), F32)],
        compiler_params=_params(2),
        name="inproj_rope",
    )(x, g, w_in_b, cos, sin)


def _attn_pool_kernel(seq_len, sink_ref, x_ref, q_ref, kp_ref, kc_ref, kn_ref, vp_ref, vc_ref,
                      vn_ref, up_ref, uc_ref, un_ref, wpool_ref, pscale_ref, wout_ref, g_ref,
                      o_ref, kext, vext, uext, mask_ref, mix_ref):
    t = x_ref.shape[1]
    i = pl.program_id(1)
    n_tiles = pl.num_programs(1)

    kext[0:BLOCK] = kp_ref[0]
    kext[BLOCK:BLOCK + t] = kc_ref[0]
    kext[BLOCK + t:] = kn_ref[0]
    vext[0:BLOCK] = vp_ref[0]
    vext[BLOCK:BLOCK + t] = vc_ref[0]
    vext[BLOCK + t:] = vn_ref[0]

    row = lax.broadcasted_iota(jnp.int32, (BLOCK, BAND), 0)
    col = lax.broadcasted_iota(jnp.int32, (BLOCK, BAND), 1)

    def block_body(jb, carry):
        r0 = pl.multiple_of(jb * BLOCK, BLOCK)
        first = jnp.logical_and(i == 0, jb == 0)
        last = jnp.logical_and(i == n_tiles - 1, jb == t // BLOCK - 1)
        lo = jnp.where(first, BLOCK, 0)
        hi = jnp.where(last, 2 * BLOCK, BAND)
        valid = (col >= jnp.maximum(row, lo)) & (col <= row + 2 * WINDOW) & (col < hi)
        mask_ref[...] = jnp.where(valid, 0.0, 1.0)
        for c in range(ATTN_WIDTH // LANES):
            qc = q_ref[0, pl.ds(r0, BLOCK), c * LANES:(c + 1) * LANES]
            group = c // 2
            acc = jnp.zeros((BLOCK, LANES), F32)
            for e in range(2):
                sl = slice((2 * group + e) * LANES, (2 * group + e + 1) * LANES)
                kband = kext[pl.ds(r0, BAND), sl]
                s = lax.dot_general(qc, kband, (((1,), (1,)), ((), ())),
                                    preferred_element_type=F32)
                s = jnp.where(mask_ref[...] > 0.5, NEG_INF, s)
                sink = sink_ref[2 * c + e]
                m = jnp.maximum(jnp.max(s, axis=-1, keepdims=True), sink)
                p = jnp.exp(s - m)
                denom = jnp.sum(p, axis=-1, keepdims=True) + jnp.exp(sink - m)
                vband = vext[pl.ds(r0, BAND), sl]
                o = jnp.dot(p.astype(BF16), vband, preferred_element_type=F32)
                acc = acc + o / denom
            mix_ref[pl.ds(r0, BLOCK), c * LANES:(c + 1) * LANES] = acc.astype(BF16)
        return carry

    lax.fori_loop(0, t // BLOCK, block_body, 0)

    uext[0:POOL_HALO] = jnp.where(i > 0, up_ref[0], 0.0)
    uext[POOL_HALO:POOL_HALO + t] = uc_ref[0]
    uext[POOL_HALO + t:] = jnp.where(i < n_tiles - 1, un_ref[0], 0.0)
    pos = i * t + lax.broadcasted_iota(jnp.int32, (t, 1), 0)
    for gi, w in enumerate(POOL_WINDOWS):
        half = w // 2
        lanes = slice(gi * POOL_GROUP_DIM, (gi + 1) * POOL_GROUP_DIM)
        wsum = uext[POOL_HALO - half:POOL_HALO - half + t, lanes]
        for d in range(-half + 1, half):
            wsum = wsum + uext[POOL_HALO + d:POOL_HALO + d + t, lanes]
        cnt = (jnp.minimum(pos + half, seq_len) - jnp.maximum(pos - half, 0)).astype(F32)
        pooled = wsum / cnt - uc_ref[0, :, lanes]
        mixed = jnp.dot(pooled.astype(BF16), wpool_ref[gi], preferred_element_type=F32)
        mixed = mixed * pscale_ref[:, lanes]
        mix_ref[:, ATTN_WIDTH + gi * POOL_GROUP_DIM:ATTN_WIDTH + (gi + 1) * POOL_GROUP_DIM] = (
            mixed.astype(BF16))

    m = jnp.dot(mix_ref[...], wout_ref[...], preferred_element_type=F32)
    o_ref[0] = x_ref[0] + _rmsnorm(m, g_ref[...])


def _attn_pool(x, q, k4, v4, u, sink, w_pool_b, pool_scale, w_out_b, g_post):
    b, s, _ = x.shape
    t = TOKEN_TILE
    per_blk = t // BLOCK
    n_blk = s // BLOCK
    per_halo = t // POOL_HALO
    n_halo = s // POOL_HALO
    tok = lambda width: pl.BlockSpec((1, t, width), lambda bi, i: (bi, i, 0))
    blk_prev = pl.BlockSpec((1, BLOCK, 4 * LANES),
                            lambda bi, i: (bi, jnp.maximum(i * per_blk - 1, 0), 0))
    blk_next = pl.BlockSpec((1, BLOCK, 4 * LANES),
                            lambda bi, i: (bi, jnp.minimum((i + 1) * per_blk, n_blk - 1), 0))
    halo_prev = pl.BlockSpec((1, POOL_HALO, POOL_WIDTH),
                             lambda bi, i: (bi, jnp.maximum(i * per_halo - 1, 0), 0))
    halo_next = pl.BlockSpec((1, POOL_HALO, POOL_WIDTH),
                             lambda bi, i: (bi, jnp.minimum((i + 1) * per_halo, n_halo - 1), 0))
    return pl.pallas_call(
        functools.partial(_attn_pool_kernel, s),
        grid=(b, s // t),
        in_specs=[pl.BlockSpec(memory_space=pltpu.SMEM),
                  tok(D_MODEL), tok(ATTN_WIDTH),
                  blk_prev, tok(4 * LANES), blk_next,
                  blk_prev, tok(4 * LANES), blk_next,
                  halo_prev, tok(POOL_WIDTH), halo_next,
                  _const_spec((len(POOL_WINDOWS), POOL_GROUP_DIM, POOL_GROUP_DIM)),
                  _const_spec((1, POOL_WIDTH)),
                  _const_spec((ATTN_WIDTH + POOL_WIDTH, D_MODEL)),
                  _const_spec((1, D_MODEL))],
        out_specs=tok(D_MODEL),
        out_shape=jax.ShapeDtypeStruct((b, s, D_MODEL), F32),
        scratch_shapes=[pltpu.VMEM((t + 2 * BLOCK, 4 * LANES), BF16),
                        pltpu.VMEM((t + 2 * BLOCK, 4 * LANES), BF16),
                        pltpu.VMEM((t + 2 * POOL_HALO, POOL_WIDTH), F32),
                        pltpu.VMEM((BLOCK, BAND), F32),
                        pltpu.VMEM((t, ATTN_WIDTH + POOL_WIDTH), BF16)],
        compiler_params=_params(2),
        name="attn_pool_outproj",
    )(sink, x, q, k4, k4, k4, v4, v4, v4, u, u, u, w_pool_b, pool_scale, w_out_b, g_post)


def _swiglu_kernel(x_ref, gpre_ref, wg_ref, wu_ref, wd_ref, gpost_ref, o_ref, act_ref):
    x = x_ref[...]
    hb = _rmsnorm(x, gpre_ref[...]).astype(BF16)
    for c in range(D_FF // FF_CHUNK):
        gate = jnp.dot(hb, wg_ref[c], preferred_element_type=F32)
        up = jnp.dot(hb, wu_ref[c], preferred_element_type=F32)
        act = (gate * jax.nn.sigmoid(gate)) * up
        act_ref[:, c * FF_CHUNK:(c + 1) * FF_CHUNK] = act.astype(BF16)
    f = jnp.dot(act_ref[...], wd_ref[...], preferred_element_type=F32)
    o_ref[...] = x + _rmsnorm(f, gpost_ref[...])


def _swiglu(x2d, g_pre, wg_b, wu_b, wd_b, g_post):
    n, _ = x2d.shape
    t = TOKEN_TILE
    n_ch = D_FF // FF_CHUNK
    tok = pl.BlockSpec((t, D_MODEL), lambda i: (i, 0))
    return pl.pallas_call(
        _swiglu_kernel,
        grid=(n // t,),
        in_specs=[tok, _const_spec((1, D_MODEL)),
                  _const_spec((n_ch, D_MODEL, FF_CHUNK)), _const_spec((n_ch, D_MODEL, FF_CHUNK)),
                  _const_spec((D_FF, D_MODEL)), _const_spec((1, D_MODEL))],
        out_specs=tok,
        out_shape=jax.ShapeDtypeStruct((n, D_MODEL), F32),
        scratch_shapes=[pltpu.VMEM((t, D_FF), BF16)],
        compiler_params=_params(1),
        name="swiglu",
    )(x2d, g_pre, wg_b, wu_b, wd_b, g_post)


def _conv_kernel(xp_ref, xc_ref, xn_ref, gpre_ref, w1_ref, b1_ref, wdw_ref, bdw_ref, lng_ref,
                 lnb_ref, w2_ref, b2_ref, gpost_ref, o_ref, gch, ych):
    t = xc_ref.shape[1]
    i = pl.program_id(1)
    n_tiles = pl.num_programs(1)
    gpre = gpre_ref[...]
    x = xc_ref[0]
    hcat = jnp.concatenate([_rmsnorm(xp_ref[0], gpre).astype(BF16),
                            _rmsnorm(x, gpre).astype(BF16),
                            _rmsnorm(xn_ref[0], gpre).astype(BF16)], axis=0)
    z = jnp.dot(hcat, w1_ref[...], preferred_element_type=F32) + b1_ref[...]
    glu = z[:, :D_MODEL] * jax.nn.sigmoid(z[:, D_MODEL:])
    rowid = lax.broadcasted_iota(jnp.int32, (t + 2 * CONV_HALO, 1), 0)
    first_row = jnp.where(i > 0, 0, CONV_HALO)
    end_row = jnp.where(i < n_tiles - 1, t + 2 * CONV_HALO, t + CONV_HALO)
    glu = jnp.where((rowid >= first_row) & (rowid < end_row), glu, 0.0)
    for c in range(N_CHUNKS):
        gch[c] = glu[:, c * LANES:(c + 1) * LANES]

    def chunk_body(c, carry):
        wk = wdw_ref[c]
        bias = bdw_ref[c]
        for rb in range(t // CONV_ROWS):
            base = rb * CONV_ROWS + CONV_HALO - CONV_WIDTH // 2
            acc = jnp.broadcast_to(bias, (CONV_ROWS, LANES))
            for k in range(CONV_WIDTH):
                acc = acc + gch[c, base + k:base + k + CONV_ROWS, :] * wk[k:k + 1, :]
            ych[c, rb * CONV_ROWS:(rb + 1) * CONV_ROWS, :] = acc
        return carry

    lax.fori_loop(0, N_CHUNKS, chunk_body, 0)

    y = jnp.concatenate([ych[c] for c in range(N_CHUNKS)], axis=1)
    mu = jnp.mean(y, axis=-1, keepdims=True)
    var = jnp.mean(jnp.square(y - mu), axis=-1, keepdims=True)
    yn = (y - mu) * lax.rsqrt(var + EPS) * lng_ref[...] + lnb_ref[...]
    act = (yn * jax.nn.sigmoid(yn)).astype(BF16)
    m = jnp.dot(act, w2_ref[...], preferred_element_type=F32) + b2_ref[...]
    o_ref[0] = x + _rmsnorm(m, gpost_ref[...])


def _conv_module(x, g_pre, w1_b, b1, wdw, bdw, ln_g, ln_b, w2_b, b2, g_post):
    b, s, _ = x.shape
    t = TOKEN_TILE
    per_halo = t // CONV_HALO
    n_halo = s // CONV_HALO
    tok = pl.BlockSpec((1, t, D_MODEL), lambda bi, i: (bi, i, 0))
    halo_prev = pl.BlockSpec((1, CONV_HALO, D_MODEL),
                             lambda bi, i: (bi, jnp.maximum(i * per_halo - 1, 0), 0))
    halo_next = pl.BlockSpec((1, CONV_HALO, D_MODEL),
                             lambda bi, i: (bi, jnp.minimum((i + 1) * per_halo, n_halo - 1), 0))
    return pl.pallas_call(
        _conv_kernel,
        grid=(b, s // t),
        in_specs=[halo_prev, tok, halo_next,
                  _const_spec((1, D_MODEL)),
                  _const_spec((D_MODEL, 2 * D_MODEL)), _const_spec((1, 2 * D_MODEL)),
                  _const_spec((N_CHUNKS, CONV_WIDTH, LANES)), _const_spec((N_CHUNKS, 1, LANES)),
                  _const_spec((1, D_MODEL)), _const_spec((1, D_MODEL)),
                  _const_spec((D_MODEL, D_MODEL)), _const_spec((1, D_MODEL)),
                  _const_spec((1, D_MODEL))],
        out_specs=tok,
        out_shape=jax.ShapeDtypeStruct((b, s, D_MODEL), F32),
        scratch_shapes=[pltpu.VMEM((N_CHUNKS, t + 2 * CONV_HALO, LANES), F32),
                        pltpu.VMEM((N_CHUNKS, t, LANES), F32)],
        compiler_params=_params(2),
        name="conformer_conv",
    )(x, x, x, g_pre, w1_b, b1, wdw, bdw, ln_g, ln_b, w2_b, b2, g_post)


def _rope_tables(seq_len):
    half = HEAD_DIM // 2
    inv_freq = ROPE_THETA ** (-jnp.arange(0, half, dtype=F32) * 2.0 / HEAD_DIM)
    ang = jnp.arange(seq_len).astype(F32)[:, None] * inv_freq[None, :]
    cos = jnp.cos(ang)
    sin = jnp.sin(ang)
    return jnp.tile(cos, (1, 4)), jnp.concatenate([-sin, sin, -sin, sin], axis=1)


def _chunk_major(w):
    rows = w.shape[0]
    return w.reshape(rows, N_CHUNKS, LANES).transpose(1, 0, 2)


def kernel(x_prompt, x_sample, mix_pre_g, mix_post_g, ffn_pre_g, ffn_post_g, w_in, attn_sink,
           w_pool, pool_scale, w_out, conv_w_pw1, conv_b_pw1, conv_w_dw, conv_b_dw, conv_ln_g,
           conv_ln_b, conv_w_pw2, conv_b_pw2, ffn_w_gate, ffn_w_up, ffn_w_down):
    depth = mix_pre_g.shape[0]
    n_ff = D_FF // FF_CHUNK
    row = lambda v: v.reshape(1, -1)
    col_scale = jnp.concatenate([jnp.full((ATTN_WIDTH,), HEAD_DIM ** -0.5, F32),
                                 jnp.ones((IN_PROJ_WIDTH - ATTN_WIDTH,), F32)])
    w_in_b = (w_in * col_scale).astype(BF16)
    w_pool_b = w_pool.astype(BF16)
    w_out_b = w_out.astype(BF16)
    w1_b = conv_w_pw1.astype(BF16)
    w2_b = conv_w_pw2.astype(BF16)
    ff_cols = lambda w: w.astype(BF16).reshape(depth, D_MODEL, n_ff, FF_CHUNK).transpose(0, 2, 1, 3)
    wg_b = ff_cols(ffn_w_gate)
    wu_b = ff_cols(ffn_w_up)
    wd_b = ffn_w_down.astype(BF16)

    def trunk(x):
        b, s, _ = x.shape
        cos, sin = _rope_tables(s)
        for layer in range(depth):
            if layer % 2 == 0:
                e = layer // 2
                q, k4, v4, u = _inproj(x, row(mix_pre_g[layer]), w_in_b[e], cos, sin)
                x = _attn_pool(x, q, k4, v4, u, attn_sink[e], w_pool_b[e], row(pool_scale[e]),
                               w_out_b[e], row(mix_post_g[layer]))
            else:
                o = layer // 2
                x = _conv_module(x, row(mix_pre_g[layer]), w1_b[o], row(conv_b_pw1[o]),
                                 _chunk_major(conv_w_dw[o]),
                                 conv_b_dw[o].reshape(N_CHUNKS, 1, LANES),
                                 row(conv_ln_g[o]), row(conv_ln_b[o]), w2_b[o],
                                 row(conv_b_pw2[o]), row(mix_post_g[layer]))
            x = _swiglu(x.reshape(b * s, D_MODEL), row(ffn_pre_g[layer]), wg_b[layer],
                        wu_b[layer], wd_b[layer], row(ffn_post_g[layer])).reshape(b, s, D_MODEL)
        return x

    return (trunk(x_prompt), trunk(x_sample))
```

```python
import functools

import jax
import jax.numpy as jnp
from jax import lax
from jax.experimental import pallas as pl
from jax.experimental.pallas import tpu as pltpu

D_MODEL = 1024
HEAD_DIM = 64
N_Q_HEADS = 8
N_KV_HEADS = 2
ATTN_WIDTH = N_Q_HEADS * HEAD_DIM
KV_WIDTH = N_KV_HEADS * HEAD_DIM
WINDOW = 128
BLOCK = 128
BAND = 3 * BLOCK
ROPE_THETA = 10000.0
POOL_WINDOWS = (2, 4, 8, 16)
POOL_GROUP_DIM = 128
POOL_WIDTH = 512
POOL_HALO = 8
IN_PROJ_WIDTH = ATTN_WIDTH + 2 * KV_WIDTH + POOL_WIDTH
CONV_WIDTH = 31
CONV_HALO = 16
D_FF = 2816
EPS = 1e-6
NEG_INF = -1e30

LANES = 128
N_CHUNKS = D_MODEL // LANES
TOKEN_TILE = 512
FF_CHUNK = 256
CONV_ROWS = 64
VMEM_LIMIT_BYTES = 56 * 1024 * 1024

F32 = jnp.float32
BF16 = jnp.bfloat16


def _rmsnorm(x, g):
    ms = jnp.mean(x * x, axis=-1, keepdims=True)
    return (x * lax.rsqrt(ms + EPS)) * g


def _params(n_axes):
    return pltpu.CompilerParams(
        dimension_semantics=("parallel",) * n_axes, vmem_limit_bytes=VMEM_LIMIT_BYTES)


def _const_spec(shape):
    zeros = (0,) * len(shape)
    return pl.BlockSpec(shape, lambda *_: zeros, pipeline_mode=pl.Buffered(1))


def _inproj_kernel(x_ref, g_ref, w_ref, cos_ref, sin_ref, q_ref, k4_ref, v4_ref, u_ref):
    t = x_ref.shape[1]
    h = _rmsnorm(x_ref[0], g_ref[...])
    z = jnp.dot(h.astype(BF16), w_ref[...], preferred_element_type=F32)
    cos = cos_ref[...]
    sin = sin_ref[...]
    lane = lax.broadcasted_iota(jnp.int32, (t, LANES), 1)
    first_half = (lane & (HEAD_DIM - 1)) < HEAD_DIM // 2
    low = lane < HEAD_DIM

    def rope(zc):
        swapped = jnp.where(first_half, pltpu.roll(zc, LANES - HEAD_DIM // 2, 1),
                            pltpu.roll(zc, HEAD_DIM // 2, 1))
        return zc * cos + swapped * sin

    def pad_heads(a):
        ar = pltpu.roll(a, HEAD_DIM, 1)
        zero = jnp.zeros_like(a)
        return (jnp.where(low, a, zero), jnp.where(low, zero, ar),
                jnp.where(low, ar, zero), jnp.where(low, zero, a))

    for c in range(ATTN_WIDTH // LANES):
        q_ref[0, :, c * LANES:(c + 1) * LANES] = rope(z[:, c * LANES:(c + 1) * LANES]).astype(BF16)
    k = rope(z[:, ATTN_WIDTH:ATTN_WIDTH + KV_WIDTH])
    v = z[:, ATTN_WIDTH + KV_WIDTH:ATTN_WIDTH + 2 * KV_WIDTH]
    for j, (kp, vp) in enumerate(zip(pad_heads(k), pad_heads(v))):
        k4_ref[0, :, j * LANES:(j + 1) * LANES] = kp.astype(BF16)
        v4_ref[0, :, j * LANES:(j + 1) * LANES] = vp.astype(BF16)
    u_ref[0] = z[:, ATTN_WIDTH + 2 * KV_WIDTH:]


def _inproj(x, g, w_in_b, cos, sin):
    b, s, _ = x.shape
    t = TOKEN_TILE
    tok = lambda width: pl.BlockSpec((1, t, width), lambda bi, i: (bi, i, 0))
    table = pl.BlockSpec((t, LANES), lambda bi, i: (i, 0))
    return pl.pallas_call(
        _inproj_kernel,
        grid=(b, s // t),
        in_specs=[tok(D_MODEL), _const_spec((1, D_MODEL)), _const_spec((D_MODEL, IN_PROJ_WIDTH)),
                  table, table],
        out_specs=[tok(ATTN_WIDTH), tok(4 * LANES), tok(4 * LANES), tok(POOL_WIDTH)],
        out_shape=[jax.ShapeDtypeStruct((b, s, ATTN_WIDTH), BF16),
                   jax.ShapeDtypeStruct((b, s, 4 * LANES), BF16),
                   jax.ShapeDtypeStruct((b, s, 4 * LANES), BF16),
                   jax.ShapeDtypeStruct((b, s, POOL_WIDTH), F32)],
        compiler_params=_params(2),
        name="inproj_rope",
    )(x, g, w_in_b, cos, sin)


def _attn_pool_kernel(seq_len, sink_ref, x_ref, q_ref, kp_ref, kc_ref, kn_ref, vp_ref, vc_ref,
                      vn_ref, up_ref, uc_ref, un_ref, wpool_ref, pscale_ref, wout_ref, g_ref,
                      o_ref, kext, vext, uext, tri_ref, s_ref, p_ref, mix_ref):
    t = x_ref.shape[1]
    i = pl.program_id(1)
    n_tiles = pl.num_programs(1)
    n_blocks = t // BLOCK

    kext[0:BLOCK] = kp_ref[0]
    kext[BLOCK:BLOCK + t] = kc_ref[0]
    kext[BLOCK + t:] = kn_ref[0]
    vext[0:BLOCK] = vp_ref[0]
    vext[BLOCK:BLOCK + t] = vc_ref[0]
    vext[BLOCK + t:] = vn_ref[0]

    row = lax.broadcasted_iota(jnp.int32, (BLOCK, BLOCK), 0)
    col = lax.broadcasted_iota(jnp.int32, (BLOCK, BLOCK), 1)
    tri_ref[0] = jnp.where(col < row, 1.0, 0.0)
    tri_ref[1] = jnp.where(col > row, 1.0, 0.0)

    for jb in range(n_blocks):
        r0 = jb * BLOCK
        slot = jb % 2
        thr_lo = jnp.where(i == 0, -1.0, 0.5) if jb == 0 else 0.5
        thr_hi = jnp.where(i == n_tiles - 1, -1.0, 0.5) if jb == n_blocks - 1 else 0.5

        for ge in range(4):
            group = ge // 2
            q2 = jnp.concatenate(
                [q_ref[0, r0:r0 + BLOCK, (2 * group + cc) * LANES:(2 * group + cc + 1) * LANES]
                 for cc in range(2)], axis=0)
            kband = kext[r0:r0 + BAND, ge * LANES:(ge + 1) * LANES]
            s_ref[slot, ge] = lax.dot_general(q2, kband, (((1,), (1,)), ((), ())),
                                              preferred_element_type=F32)

        rinv = {}
        for ge in range(4):
            group, e = ge // 2, ge % 2
            for cc in range(2):
                rows = slice(cc * BLOCK, (cc + 1) * BLOCK)
                sink = sink_ref[2 * (2 * group + cc) + e]
                s0 = jnp.where(tri_ref[0] > thr_lo, NEG_INF, s_ref[slot, ge, rows, 0:BLOCK])
                s1 = s_ref[slot, ge, rows, BLOCK:2 * BLOCK]
                s2 = jnp.where(tri_ref[1] > thr_hi, NEG_INF, s_ref[slot, ge, rows, 2 * BLOCK:])
                m = jnp.max(jnp.maximum(jnp.maximum(s0, s1), s2), axis=-1, keepdims=True)
                m = jnp.maximum(m, sink)
                p0 = jnp.exp(s0 - m)
                p1 = jnp.exp(s1 - m)
                p2 = jnp.exp(s2 - m)
                denom = jnp.sum(p0 + p1 + p2, axis=-1, keepdims=True) + jnp.exp(sink - m)
                p_ref[slot, ge, rows, 0:BLOCK] = p0.astype(BF16)
                p_ref[slot, ge, rows, BLOCK:2 * BLOCK] = p1.astype(BF16)
                p_ref[slot, ge, rows, 2 * BLOCK:] = p2.astype(BF16)
                rinv[ge, cc] = 1.0 / denom

        for group in range(2):
            acc = [None, None]
            for e in range(2):
                ge = 2 * group + e
                vband = vext[r0:r0 + BAND, ge * LANES:(ge + 1) * LANES]
                o2 = jnp.dot(p_ref[slot, ge], vband, preferred_element_type=F32)
                for cc in range(2):
                    o = o2[cc * BLOCK:(cc + 1) * BLOCK] * rinv[ge, cc]
                    acc[cc] = o if acc[cc] is None else acc[cc] + o
            for cc in range(2):
                c = 2 * group + cc
                mix_ref[r0:r0 + BLOCK, c * LANES:(c + 1) * LANES] = acc[cc].astype(BF16)

    for gi, w in enumerate(POOL_WINDOWS):
        half = w // 2
        lanes = slice(gi * POOL_GROUP_DIM, (gi + 1) * POOL_GROUP_DIM)
        uext[gi, 0:POOL_HALO] = jnp.where(i > 0, up_ref[0, :, lanes], 0.0)
        uext[gi, POOL_HALO:POOL_HALO + t] = uc_ref[0, :, lanes]
        uext[gi, POOL_HALO + t:] = jnp.where(i < n_tiles - 1, un_ref[0, :, lanes], 0.0)
        wsum = uext[gi, POOL_HALO - half:POOL_HALO - half + t]
        for d in range(-half + 1, half):
            wsum = wsum + uext[gi, POOL_HALO + d:POOL_HALO + d + t]
        centre = uc_ref[0, :, lanes]

        def edge(r):
            pos = i * t + r + lax.broadcasted_iota(jnp.int32, (POOL_HALO, LANES), 0)
            cnt = jnp.minimum(pos + half, seq_len) - jnp.maximum(pos - half, 0)
            return wsum[r:r + POOL_HALO] / cnt.astype(F32) - centre[r:r + POOL_HALO]

        inner = slice(POOL_HALO, t - POOL_HALO)
        pooled = jnp.concatenate(
            [edge(0), wsum[inner] * (1.0 / w) - centre[inner], edge(t - POOL_HALO)], axis=0)
        mixed = jnp.dot(pooled.astype(BF16), wpool_ref[gi], preferred_element_type=F32)
        mixed = mixed * pscale_ref[:, lanes]
        mix_ref[:, ATTN_WIDTH + gi * POOL_GROUP_DIM:ATTN_WIDTH + (gi + 1) * POOL_GROUP_DIM] = (
            mixed.astype(BF16))

    m = jnp.dot(mix_ref[...], wout_ref[...], preferred_element_type=F32)
    o_ref[0] = x_ref[0] + _rmsnorm(m, g_ref[...])


def _attn_pool(x, q, k4, v4, u, sink, w_pool_b, pool_scale, w_out_b, g_post):
    b, s, _ = x.shape
    t = TOKEN_TILE
    per_blk = t // BLOCK
    n_blk = s // BLOCK
    per_halo = t // POOL_HALO
    n_halo = s // POOL_HALO
    tok = lambda width: pl.BlockSpec((1, t, width), lambda bi, i: (bi, i, 0))
    blk_prev = pl.BlockSpec((1, BLOCK, 4 * LANES),
                            lambda bi, i: (bi, jnp.maximum(i * per_blk - 1, 0), 0))
    blk_next = pl.BlockSpec((1, BLOCK, 4 * LANES),
                            lambda bi, i: (bi, jnp.minimum((i + 1) * per_blk, n_blk - 1), 0))
    halo_prev = pl.BlockSpec((1, POOL_HALO, POOL_WIDTH),
                             lambda bi, i: (bi, jnp.maximum(i * per_halo - 1, 0), 0))
    halo_next = pl.BlockSpec((1, POOL_HALO, POOL_WIDTH),
                             lambda bi, i: (bi, jnp.minimum((i + 1) * per_halo, n_halo - 1), 0))
    return pl.pallas_call(
        functools.partial(_attn_pool_kernel, s),
        grid=(b, s // t),
        in_specs=[pl.BlockSpec(memory_space=pltpu.SMEM),
                  tok(D_MODEL), tok(ATTN_WIDTH),
                  blk_prev, tok(4 * LANES), blk_next,
                  blk_prev, tok(4 * LANES), blk_next,
                  halo_prev, tok(POOL_WIDTH), halo_next,
                  _const_spec((len(POOL_WINDOWS), POOL_GROUP_DIM, POOL_GROUP_DIM)),
                  _const_spec((1, POOL_WIDTH)),
                  _const_spec((ATTN_WIDTH + POOL_WIDTH, D_MODEL)),
                  _const_spec((1, D_MODEL))],
        out_specs=tok(D_MODEL),
        out_shape=jax.ShapeDtypeStruct((b, s, D_MODEL), F32),
        scratch_shapes=[pltpu.VMEM((t + 2 * BLOCK, 4 * LANES), BF16),
                        pltpu.VMEM((t + 2 * BLOCK, 4 * LANES), BF16),
                        pltpu.VMEM((len(POOL_WINDOWS), t + 2 * POOL_HALO, LANES), F32),
                        pltpu.VMEM((2, BLOCK, BLOCK), F32),
                        pltpu.VMEM((2, 4, 2 * BLOCK, BAND), F32),
                        pltpu.VMEM((2, 4, 2 * BLOCK, BAND), BF16),
                        pltpu.VMEM((t, ATTN_WIDTH + POOL_WIDTH), BF16)],
        compiler_params=_params(2),
        name="attn_pool_outproj",
    )(sink, x, q, k4, k4, k4, v4, v4, v4, u, u, u, w_pool_b, pool_scale, w_out_b, g_post)


def _swiglu_kernel(x_ref, gpre_ref, wg_ref, wu_ref, wd_ref, gpost_ref, o_ref, act_ref):
    x = x_ref[...]
    hb = _rmsnorm(x, gpre_ref[...]).astype(BF16)
    for c in range(D_FF // FF_CHUNK):
        cols = slice(c * FF_CHUNK, (c + 1) * FF_CHUNK)
        gate = jnp.dot(hb, wg_ref[:, cols], preferred_element_type=F32)
        up = jnp.dot(hb, wu_ref[:, cols], preferred_element_type=F32)
        act = (gate * jax.nn.sigmoid(gate)) * up
        act_ref[:, cols] = act.astype(BF16)
    f = jnp.dot(act_ref[...], wd_ref[...], preferred_element_type=F32)
    o_ref[...] = x + _rmsnorm(f, gpost_ref[...])


def _swiglu(x2d, g_pre, wg_b, wu_b, wd_b, g_post):
    n, _ = x2d.shape
    t = TOKEN_TILE
    tok = pl.BlockSpec((t, D_MODEL), lambda i: (i, 0))
    return pl.pallas_call(
        _swiglu_kernel,
        grid=(n // t,),
        in_specs=[tok, _const_spec((1, D_MODEL)),
                  _const_spec((D_MODEL, D_FF)), _const_spec((D_MODEL, D_FF)),
                  _const_spec((D_FF, D_MODEL)), _const_spec((1, D_MODEL))],
        out_specs=tok,
        out_shape=jax.ShapeDtypeStruct((n, D_MODEL), F32),
        scratch_shapes=[pltpu.VMEM((t, D_FF), BF16)],
        compiler_params=_params(1),
        name="swiglu",
    )(x2d, g_pre, wg_b, wu_b, wd_b, g_post)


def _conv_kernel(xp_ref, xc_ref, xn_ref, gpre_ref, w1_ref, b1_ref, wdw_ref, bdw_ref, lng_ref,
                 lnb_ref, w2_ref, b2_ref, gpost_ref, o_ref, gch, ych):
    t = xc_ref.shape[1]
    i = pl.program_id(1)
    n_tiles = pl.num_programs(1)
    gpre = gpre_ref[...]
    x = xc_ref[0]
    hcat = jnp.concatenate([_rmsnorm(xp_ref[0], gpre).astype(BF16),
                            _rmsnorm(x, gpre).astype(BF16),
                            _rmsnorm(xn_ref[0], gpre).astype(BF16)], axis=0)
    z = jnp.dot(hcat, w1_ref[...], preferred_element_type=F32) + b1_ref[...]
    glu = z[:, :D_MODEL] * jax.nn.sigmoid(z[:, D_MODEL:])
    rowid = lax.broadcasted_iota(jnp.int32, (t + 2 * CONV_HALO, 1), 0)
    first_row = jnp.where(i > 0, 0, CONV_HALO)
    end_row = jnp.where(i < n_tiles - 1, t + 2 * CONV_HALO, t + CONV_HALO)
    glu = jnp.where((rowid >= first_row) & (rowid < end_row), glu, 0.0)
    for c in range(N_CHUNKS):
        gch[c] = glu[:, c * LANES:(c + 1) * LANES]

    def chunk_body(c, carry):
        wk = wdw_ref[c]
        bias = bdw_ref[c]
        for rb in range(t // CONV_ROWS):
            base = rb * CONV_ROWS + CONV_HALO - CONV_WIDTH // 2
            acc = jnp.broadcast_to(bias, (CONV_ROWS, LANES))
            for k in range(CONV_WIDTH):
                acc = acc + gch[c, base + k:base + k + CONV_ROWS, :] * wk[k:k + 1, :]
            ych[c, rb * CONV_ROWS:(rb + 1) * CONV_ROWS, :] = acc
        return carry

    lax.fori_loop(0, N_CHUNKS, chunk_body, 0)

    y = jnp.concatenate([ych[c] for c in range(N_CHUNKS)], axis=1)
    mu = jnp.mean(y, axis=-1, keepdims=True)
    var = jnp.mean(jnp.square(y - mu), axis=-1, keepdims=True)
    yn = (y - mu) * lax.rsqrt(var + EPS) * lng_ref[...] + lnb_ref[...]
    act = (yn * jax.nn.sigmoid(yn)).astype(BF16)
    m = jnp.dot(act, w2_ref[...], preferred_element_type=F32) + b2_ref[...]
    o_ref[0] = x + _rmsnorm(m, gpost_ref[...])


def _conv_module(x, g_pre, w1_b, b1, wdw, bdw, ln_g, ln_b, w2_b, b2, g_post):
    b, s, _ = x.shape
    t = TOKEN_TILE
    per_halo = t // CONV_HALO
    n_halo = s // CONV_HALO
    tok = pl.BlockSpec((1, t, D_MODEL), lambda bi, i: (bi, i, 0))
    halo_prev = pl.BlockSpec((1, CONV_HALO, D_MODEL),
                             lambda bi, i: (bi, jnp.maximum(i * per_halo - 1, 0), 0))
    halo_next = pl.BlockSpec((1, CONV_HALO, D_MODEL),
                             lambda bi, i: (bi, jnp.minimum((i + 1) * per_halo, n_halo - 1), 0))
    return pl.pallas_call(
        _conv_kernel,
        grid=(b, s // t),
        in_specs=[halo_prev, tok, halo_next,
                  _const_spec((1, D_MODEL)),
                  _const_spec((D_MODEL, 2 * D_MODEL)), _const_spec((1, 2 * D_MODEL)),
                  _const_spec((N_CHUNKS, CONV_WIDTH, LANES)), _const_spec((N_CHUNKS, 1, LANES)),
                  _const_spec((1, D_MODEL)), _const_spec((1, D_MODEL)),
                  _const_spec((D_MODEL, D_MODEL)), _const_spec((1, D_MODEL)),
                  _const_spec((1, D_MODEL))],
        out_specs=tok,
        out_shape=jax.ShapeDtypeStruct((b, s, D_MODEL), F32),
        scratch_shapes=[pltpu.VMEM((N_CHUNKS, t + 2 * CONV_HALO, LANES), F32),
                        pltpu.VMEM((N_CHUNKS, t, LANES), F32)],
        compiler_params=_params(2),
        name="conformer_conv",
    )(x, x, x, g_pre, w1_b, b1, wdw, bdw, ln_g, ln_b, w2_b, b2, g_post)


def _rope_tables(seq_len):
    half = HEAD_DIM // 2
    inv_freq = ROPE_THETA ** (-jnp.arange(0, half, dtype=F32) * 2.0 / HEAD_DIM)
    ang = jnp.arange(seq_len).astype(F32)[:, None] * inv_freq[None, :]
    cos = jnp.cos(ang)
    sin = jnp.sin(ang)
    return jnp.tile(cos, (1, 4)), jnp.concatenate([-sin, sin, -sin, sin], axis=1)


def _chunk_major(w):
    rows = w.shape[0]
    return w.reshape(rows, N_CHUNKS, LANES).transpose(1, 0, 2)


def kernel(x_prompt, x_sample, mix_pre_g, mix_post_g, ffn_pre_g, ffn_post_g, w_in, attn_sink,
           w_pool, pool_scale, w_out, conv_w_pw1, conv_b_pw1, conv_w_dw, conv_b_dw, conv_ln_g,
           conv_ln_b, conv_w_pw2, conv_b_pw2, ffn_w_gate, ffn_w_up, ffn_w_down):
    depth = mix_pre_g.shape[0]
    row = lambda v: v.reshape(1, -1)
    col_scale = jnp.concatenate([jnp.full((ATTN_WIDTH,), HEAD_DIM ** -0.5, F32),
                                 jnp.ones((IN_PROJ_WIDTH - ATTN_WIDTH,), F32)])
    w_in_b = (w_in * col_scale).astype(BF16)
    w_pool_b = w_pool.astype(BF16)
    w_out_b = w_out.astype(BF16)
    w1_b = conv_w_pw1.astype(BF16)
    w2_b = conv_w_pw2.astype(BF16)
    wg_b = ffn_w_gate.astype(BF16)
    wu_b = ffn_w_up.astype(BF16)
    wd_b = ffn_w_down.astype(BF16)

    def trunk(x):
        b, s, _ = x.shape
        cos, sin = _rope_tables(s)
        for layer in range(depth):
            if layer % 2 == 0:
                e = layer // 2
                q, k4, v4, u = _inproj(x, row(mix_pre_g[layer]), w_in_b[e], cos, sin)
                x = _attn_pool(x, q, k4, v4, u, attn_sink[e], w_pool_b[e], row(pool_scale[e]),
                               w_out_b[e], row(mix_post_g[layer]))
            else:
                o = layer // 2
                x = _conv_module(x, row(mix_pre_g[layer]), w1_b[o], row(conv_b_pw1[o]),
                                 _chunk_major(conv_w_dw[o]),
                                 conv_b_dw[o].reshape(N_CHUNKS, 1, LANES),
                                 row(conv_ln_g[o]), row(conv_ln_b[o]), w2_b[o],
                                 row(conv_b_pw2[o]), row(mix_post_g[layer]))
            x = _swiglu(x.reshape(b * s, D_MODEL), row(ffn_pre_g[layer]), wg_b[layer],
                        wu_b[layer], wd_b[layer], row(ffn_post_g[layer])).reshape(b, s, D_MODEL)
        return x

    return (trunk(x_prompt), trunk(x_sample))
```

```python
import functools

import jax
import jax.numpy as jnp
from jax import lax
from jax.experimental import pallas as pl
from jax.experimental.pallas import tpu as pltpu

D_MODEL = 1024
HEAD_DIM = 64
N_Q_HEADS = 8
N_KV_HEADS = 2
ATTN_WIDTH = N_Q_HEADS * HEAD_DIM
KV_WIDTH = N_KV_HEADS * HEAD_DIM
WINDOW = 128
BLOCK = 128
BAND = 3 * BLOCK
ROPE_THETA = 10000.0
POOL_WINDOWS = (2, 4, 8, 16)
POOL_GROUP_DIM = 128
POOL_WIDTH = 512
POOL_HALO = 8
IN_PROJ_WIDTH = ATTN_WIDTH + 2 * KV_WIDTH + POOL_WIDTH
CONV_WIDTH = 31
CONV_HALO = 16
D_FF = 2816
EPS = 1e-6
NEG_INF = -1e30

LANES = 128
N_CHUNKS = D_MODEL // LANES
TOKEN_TILE = 512
FF_CHUNK = 256
N_FF_CHUNKS = D_FF // FF_CHUNK
CONV_ROWS = 64
VMEM_LIMIT_BYTES = 56 * 1024 * 1024

F32 = jnp.float32
BF16 = jnp.bfloat16


def _rmsnorm(x, g):
    ms = jnp.mean(x * x, axis=-1, keepdims=True)
    return (x * lax.rsqrt(ms + EPS)) * g


def _params(semantics):
    return pltpu.CompilerParams(dimension_semantics=semantics, vmem_limit_bytes=VMEM_LIMIT_BYTES)


def _const_spec(shape):
    zeros = (0,) * len(shape)
    return pl.BlockSpec(shape, lambda *_: zeros, pipeline_mode=pl.Buffered(1))


def _ffn_begin(j, x1_buf, fpre_ref, hb_ref):
    t = hb_ref.shape[0]

    @pl.when(j == 0)
    def _():
        x1_buf[1] = jnp.zeros((t, D_MODEL), F32)

    hb_ref[...] = _rmsnorm(x1_buf[1 - lax.rem(j, 2)], fpre_ref[...]).astype(BF16)


def _ffn_chunk(c, hb_ref, wg_ref, wu_ref, act_ref):
    start = c * FF_CHUNK
    cols = pl.ds(start if isinstance(c, int) else pl.multiple_of(start, FF_CHUNK), FF_CHUNK)
    gate = jnp.dot(hb_ref[...], wg_ref[:, cols], preferred_element_type=F32)
    up = jnp.dot(hb_ref[...], wu_ref[:, cols], preferred_element_type=F32)
    act_ref[:, cols] = ((gate * jax.nn.sigmoid(gate)) * up).astype(BF16)


def _ffn_finish(j, x1_buf, act_ref, wd_ref, fpost_ref, o_ref):
    f = jnp.dot(act_ref[...], wd_ref[...], preferred_element_type=F32)
    o_ref[0] = x1_buf[1 - lax.rem(j, 2)] + _rmsnorm(f, fpost_ref[...])


_FFN_SPECS = (_const_spec((1, D_MODEL)), _const_spec((D_MODEL, D_FF)), _const_spec((D_MODEL, D_FF)),
              _const_spec((D_FF, D_MODEL)), _const_spec((1, D_MODEL)))


def _ffn_scratch(t):
    return [pltpu.VMEM((2, t, D_MODEL), F32),
            pltpu.VMEM((t, D_MODEL), BF16),
            pltpu.VMEM((t, D_FF), BF16)]


def _tile_maps(b, s, t):
    per_seq = s // t
    n_tiles = b * per_seq

    def mixer(j):
        tile = jnp.minimum(j, n_tiles - 1)
        return tile // per_seq, tile % per_seq

    def ffn(j):
        tile = jnp.maximum(j - 1, 0)
        return tile // per_seq, tile % per_seq, 0

    return per_seq, n_tiles, mixer, ffn


def _inproj_kernel(x_ref, g_ref, w_ref, cos_ref, sin_ref, q_ref, k4_ref, v4_ref, u_ref):
    t = x_ref.shape[1]
    h = _rmsnorm(x_ref[0], g_ref[...])
    z = jnp.dot(h.astype(BF16), w_ref[...], preferred_element_type=F32)
    cos = cos_ref[...]
    sin = sin_ref[...]
    lane = lax.broadcasted_iota(jnp.int32, (t, LANES), 1)
    first_half = (lane & (HEAD_DIM - 1)) < HEAD_DIM // 2
    low = lane < HEAD_DIM

    def rope(zc):
        swapped = jnp.where(first_half, pltpu.roll(zc, LANES - HEAD_DIM // 2, 1),
                            pltpu.roll(zc, HEAD_DIM // 2, 1))
        return zc * cos + swapped * sin

    def pad_heads(a):
        ar = pltpu.roll(a, HEAD_DIM, 1)
        zero = jnp.zeros_like(a)
        return (jnp.where(low, a, zero), jnp.where(low, zero, ar),
                jnp.where(low, ar, zero), jnp.where(low, zero, a))

    for c in range(ATTN_WIDTH // LANES):
        q_ref[0, :, c * LANES:(c + 1) * LANES] = rope(z[:, c * LANES:(c + 1) * LANES]).astype(BF16)
    k = rope(z[:, ATTN_WIDTH:ATTN_WIDTH + KV_WIDTH])
    v = z[:, ATTN_WIDTH + KV_WIDTH:ATTN_WIDTH + 2 * KV_WIDTH]
    for j, (kp, vp) in enumerate(zip(pad_heads(k), pad_heads(v))):
        k4_ref[0, :, j * LANES:(j + 1) * LANES] = kp.astype(BF16)
        v4_ref[0, :, j * LANES:(j + 1) * LANES] = vp.astype(BF16)
    u_ref[0] = z[:, ATTN_WIDTH + 2 * KV_WIDTH:]


def _inproj(x, g, w_in_b, cos, sin):
    b, s, _ = x.shape
    t = TOKEN_TILE
    tok = lambda width: pl.BlockSpec((1, t, width), lambda bi, i: (bi, i, 0))
    table = pl.BlockSpec((t, LANES), lambda bi, i: (i, 0))
    return pl.pallas_call(
        _inproj_kernel,
        grid=(b, s // t),
        in_specs=[tok(D_MODEL), _const_spec((1, D_MODEL)), _const_spec((D_MODEL, IN_PROJ_WIDTH)),
                  table, table],
        out_specs=[tok(ATTN_WIDTH), tok(4 * LANES), tok(4 * LANES), tok(POOL_WIDTH)],
        out_shape=[jax.ShapeDtypeStruct((b, s, ATTN_WIDTH), BF16),
                   jax.ShapeDtypeStruct((b, s, 4 * LANES), BF16),
                   jax.ShapeDtypeStruct((b, s, 4 * LANES), BF16),
                   jax.ShapeDtypeStruct((b, s, POOL_WIDTH), F32)],
        compiler_params=_params(("parallel", "parallel")),
        name="inproj_rope",
    )(x, g, w_in_b, cos, sin)


def _attn_ffn_kernel(seq_len, tiles_per_seq, sink_ref, x_ref, q_ref, kp_ref, kc_ref, kn_ref,
                     vp_ref, vc_ref, vn_ref, up_ref, uc_ref, un_ref, wpool_ref, pscale_ref,
                     wout_ref, g_ref, fpre_ref, wg_ref, wu_ref, wd_ref, fpost_ref, o_ref,
                     kext, vext, uext, tri_ref, s_ref, p_ref, mix_ref, x1_buf, hb_ref, act_ref):
    t = x_ref.shape[1]
    j = pl.program_id(0)
    tile = jnp.minimum(j, pl.num_programs(0) - 2)
    i = lax.rem(tile, tiles_per_seq)
    n_blocks = t // BLOCK

    _ffn_begin(j, x1_buf, fpre_ref, hb_ref)

    kext[0:BLOCK] = kp_ref[0]
    kext[BLOCK:BLOCK + t] = kc_ref[0]
    kext[BLOCK + t:] = kn_ref[0]
    vext[0:BLOCK] = vp_ref[0]
    vext[BLOCK:BLOCK + t] = vc_ref[0]
    vext[BLOCK + t:] = vn_ref[0]

    row = lax.broadcasted_iota(jnp.int32, (BLOCK, BLOCK), 0)
    col = lax.broadcasted_iota(jnp.int32, (BLOCK, BLOCK), 1)
    tri_ref[0] = jnp.where(col < row, 1.0, 0.0)
    tri_ref[1] = jnp.where(col > row, 1.0, 0.0)

    def paired_block(jb, carry):
        _ffn_chunk(2 * jb, hb_ref, wg_ref, wu_ref, act_ref)
        _ffn_chunk(2 * jb + 1, hb_ref, wg_ref, wu_ref, act_ref)

        r0 = pl.multiple_of(jb * BLOCK, BLOCK)
        thr_lo = jnp.where(jnp.logical_and(i == 0, jb == 0), -1.0, 0.5)
        thr_hi = jnp.where(jnp.logical_and(i == tiles_per_seq - 1, jb == n_blocks - 1), -1.0, 0.5)

        for ge in range(4):
            group = ge // 2
            q2 = jnp.concatenate(
                [q_ref[0, pl.ds(r0, BLOCK), (2 * group + cc) * LANES:(2 * group + cc + 1) * LANES]
                 for cc in range(2)], axis=0)
            kband = kext[pl.ds(r0, BAND), ge * LANES:(ge + 1) * LANES]
            s_ref[ge] = lax.dot_general(q2, kband, (((1,), (1,)), ((), ())),
                                        preferred_element_type=F32)

        rinv = {}
        for ge in range(4):
            group, e = ge // 2, ge % 2
            for cc in range(2):
                rows = slice(cc * BLOCK, (cc + 1) * BLOCK)
                sink = sink_ref[2 * (2 * group + cc) + e]
                s0 = jnp.where(tri_ref[0] > thr_lo, NEG_INF, s_ref[ge, rows, 0:BLOCK])
                s1 = s_ref[ge, rows, BLOCK:2 * BLOCK]
                s2 = jnp.where(tri_ref[1] > thr_hi, NEG_INF, s_ref[ge, rows, 2 * BLOCK:])
                m = jnp.max(jnp.maximum(jnp.maximum(s0, s1), s2), axis=-1, keepdims=True)
                m = jnp.maximum(m, sink)
                p0 = jnp.exp(s0 - m)
                p1 = jnp.exp(s1 - m)
                p2 = jnp.exp(s2 - m)
                denom = jnp.sum(p0 + p1 + p2, axis=-1, keepdims=True) + jnp.exp(sink - m)
                p_ref[ge, rows, 0:BLOCK] = p0.astype(BF16)
                p_ref[ge, rows, BLOCK:2 * BLOCK] = p1.astype(BF16)
                p_ref[ge, rows, 2 * BLOCK:] = p2.astype(BF16)
                rinv[ge, cc] = 1.0 / denom

        for group in range(2):
            acc = [None, None]
            for e in range(2):
                ge = 2 * group + e
                vband = vext[pl.ds(r0, BAND), ge * LANES:(ge + 1) * LANES]
                o2 = jnp.dot(p_ref[ge], vband, preferred_element_type=F32)
                for cc in range(2):
                    o = o2[cc * BLOCK:(cc + 1) * BLOCK] * rinv[ge, cc]
                    acc[cc] = o if acc[cc] is None else acc[cc] + o
            for cc in range(2):
                c = 2 * group + cc
                mix_ref[pl.ds(r0, BLOCK), c * LANES:(c + 1) * LANES] = acc[cc].astype(BF16)
        return carry

    lax.fori_loop(0, n_blocks, paired_block, 0)
    for c in range(2 * n_blocks, N_FF_CHUNKS):
        _ffn_chunk(c, hb_ref, wg_ref, wu_ref, act_ref)

    for gi, w in enumerate(POOL_WINDOWS):
        half = w // 2
        lanes = slice(gi * POOL_GROUP_DIM, (gi + 1) * POOL_GROUP_DIM)
        uext[gi, 0:POOL_HALO] = jnp.where(i > 0, up_ref[0, :, lanes], 0.0)
        uext[gi, POOL_HALO:POOL_HALO + t] = uc_ref[0, :, lanes]
        uext[gi, POOL_HALO + t:] = jnp.where(i < tiles_per_seq - 1, un_ref[0, :, lanes], 0.0)
        wsum = uext[gi, POOL_HALO - half:POOL_HALO - half + t]
        for d in range(-half + 1, half):
            wsum = wsum + uext[gi, POOL_HALO + d:POOL_HALO + d + t]
        centre = uc_ref[0, :, lanes]

        def edge(r):
            pos = i * t + r + lax.broadcasted_iota(jnp.int32, (POOL_HALO, LANES), 0)
            cnt = jnp.minimum(pos + half, seq_len) - jnp.maximum(pos - half, 0)
            return wsum[r:r + POOL_HALO] / cnt.astype(F32) - centre[r:r + POOL_HALO]

        inner = slice(POOL_HALO, t - POOL_HALO)
        pooled = jnp.concatenate(
            [edge(0), wsum[inner] * (1.0 / w) - centre[inner], edge(t - POOL_HALO)], axis=0)
        mixed = jnp.dot(pooled.astype(BF16), wpool_ref[gi], preferred_element_type=F32)
        mixed = mixed * pscale_ref[:, lanes]
        mix_ref[:, ATTN_WIDTH + gi * POOL_GROUP_DIM:ATTN_WIDTH + (gi + 1) * POOL_GROUP_DIM] = (
            mixed.astype(BF16))

    m = jnp.dot(mix_ref[...], wout_ref[...], preferred_element_type=F32)
    x1_buf[lax.rem(j, 2)] = x_ref[0] + _rmsnorm(m, g_ref[...])

    _ffn_finish(j, x1_buf, act_ref, wd_ref, fpost_ref, o_ref)


def _attn_ffn(x, q, k4, v4, u, sink, w_pool_b, pool_scale, w_out_b, g_post, ffn_params):
    b, s, _ = x.shape
    t = TOKEN_TILE
    per_seq, n_tiles, mixer, ffn = _tile_maps(b, s, t)
    per_blk = t // BLOCK
    n_blk = s // BLOCK
    per_halo = t // POOL_HALO
    n_halo = s // POOL_HALO

    def cur(j):
        bi, i = mixer(j)
        return bi, i, 0

    def blk_prev(j):
        bi, i = mixer(j)
        return bi, jnp.maximum(i * per_blk - 1, 0), 0

    def blk_next(j):
        bi, i = mixer(j)
        return bi, jnp.minimum((i + 1) * per_blk, n_blk - 1), 0

    def halo_prev(j):
        bi, i = mixer(j)
        return bi, jnp.maximum(i * per_halo - 1, 0), 0

    def halo_next(j):
        bi, i = mixer(j)
        return bi, jnp.minimum((i + 1) * per_halo, n_halo - 1), 0

    tok = lambda width: pl.BlockSpec((1, t, width), cur)
    blk = lambda index_map: pl.BlockSpec((1, BLOCK, 4 * LANES), index_map)
    halo = lambda index_map: pl.BlockSpec((1, POOL_HALO, POOL_WIDTH), index_map)
    return pl.pallas_call(
        functools.partial(_attn_ffn_kernel, s, per_seq),
        grid=(n_tiles + 1,),
        in_specs=[pl.BlockSpec(memory_space=pltpu.SMEM),
                  tok(D_MODEL), tok(ATTN_WIDTH),
                  blk(blk_prev), tok(4 * LANES), blk(blk_next),
                  blk(blk_prev), tok(4 * LANES), blk(blk_next),
                  halo(halo_prev), tok(POOL_WIDTH), halo(halo_next),
                  _const_spec((len(POOL_WINDOWS), POOL_GROUP_DIM, POOL_GROUP_DIM)),
                  _const_spec((1, POOL_WIDTH)),
                  _const_spec((ATTN_WIDTH + POOL_WIDTH, D_MODEL)),
                  _const_spec((1, D_MODEL)),
                  *_FFN_SPECS],
        out_specs=pl.BlockSpec((1, t, D_MODEL), ffn),
        out_shape=jax.ShapeDtypeStruct((b, s, D_MODEL), F32),
        scratch_shapes=[pltpu.VMEM((t + 2 * BLOCK, 4 * LANES), BF16),
                        pltpu.VMEM((t + 2 * BLOCK, 4 * LANES), BF16),
                        pltpu.VMEM((len(POOL_WINDOWS), t + 2 * POOL_HALO, LANES), F32),
                        pltpu.VMEM((2, BLOCK, BLOCK), F32),
                        pltpu.VMEM((4, 2 * BLOCK, BAND), F32),
                        pltpu.VMEM((4, 2 * BLOCK, BAND), BF16),
                        pltpu.VMEM((t, ATTN_WIDTH + POOL_WIDTH), BF16),
                        *_ffn_scratch(t)],
        compiler_params=_params(("arbitrary",)),
        name="attn_ffn",
    )(sink, x, q, k4, k4, k4, v4, v4, v4, u, u, u, w_pool_b, pool_scale, w_out_b, g_post,
      *ffn_params)


def _conv_ffn_kernel(tiles_per_seq, xp_ref, xc_ref, xn_ref, gpre_ref, w1_ref, b1_ref, wdw_ref,
                     bdw_ref, lng_ref, lnb_ref, w2_ref, b2_ref, gpost_ref, fpre_ref, wg_ref, wu_ref,
                     wd_ref, fpost_ref, o_ref, hcat_ref, gch, ych, x1_buf, hb_ref, act_ref):
    t = xc_ref.shape[1]
    j = pl.program_id(0)
    tile = jnp.minimum(j, pl.num_programs(0) - 2)
    i = lax.rem(tile, tiles_per_seq)

    _ffn_begin(j, x1_buf, fpre_ref, hb_ref)

    gpre = gpre_ref[...]
    hcat_ref[0:CONV_HALO] = _rmsnorm(xp_ref[0], gpre).astype(BF16)
    hcat_ref[CONV_HALO:CONV_HALO + t] = _rmsnorm(xc_ref[0], gpre).astype(BF16)
    hcat_ref[CONV_HALO + t:] = _rmsnorm(xn_ref[0], gpre).astype(BF16)
    rowid = lax.broadcasted_iota(jnp.int32, (t + 2 * CONV_HALO, 1), 0)
    first_row = jnp.where(i > 0, 0, CONV_HALO)
    end_row = jnp.where(i < tiles_per_seq - 1, t + 2 * CONV_HALO, t + CONV_HALO)
    inside = (rowid >= first_row) & (rowid < end_row)
    for cp in range(N_CHUNKS // 2):
        vcols = slice(cp * 2 * LANES, (cp + 1) * 2 * LANES)
        gcols = slice(D_MODEL + cp * 2 * LANES, D_MODEL + (cp + 1) * 2 * LANES)
        zv = jnp.dot(hcat_ref[...], w1_ref[:, vcols], preferred_element_type=F32) + b1_ref[:, vcols]
        zg = jnp.dot(hcat_ref[...], w1_ref[:, gcols], preferred_element_type=F32) + b1_ref[:, gcols]
        glu = jnp.where(inside, zv * jax.nn.sigmoid(zg), 0.0)
        gch[2 * cp] = glu[:, :LANES]
        gch[2 * cp + 1] = glu[:, LANES:]

    def paired_chunk(c, carry):
        _ffn_chunk(c, hb_ref, wg_ref, wu_ref, act_ref)
        wk = wdw_ref[c]
        bias = bdw_ref[c]
        for rb in range(t // CONV_ROWS):
            base = rb * CONV_ROWS + CONV_HALO - CONV_WIDTH // 2
            acc = jnp.broadcast_to(bias, (CONV_ROWS, LANES))
            for k in range(CONV_WIDTH):
                acc = acc + gch[c, base + k:base + k + CONV_ROWS, :] * wk[k:k + 1, :]
            ych[c, rb * CONV_ROWS:(rb + 1) * CONV_ROWS, :] = acc
        return carry

    lax.fori_loop(0, N_CHUNKS, paired_chunk, 0)
    for c in range(N_CHUNKS, N_FF_CHUNKS):
        _ffn_chunk(c, hb_ref, wg_ref, wu_ref, act_ref)

    y = jnp.concatenate([ych[c] for c in range(N_CHUNKS)], axis=1)
    mu = jnp.mean(y, axis=-1, keepdims=True)
    var = jnp.mean(jnp.square(y - mu), axis=-1, keepdims=True)
    yn = (y - mu) * lax.rsqrt(var + EPS) * lng_ref[...] + lnb_ref[...]
    act = (yn * jax.nn.sigmoid(yn)).astype(BF16)
    m = jnp.dot(act, w2_ref[...], preferred_element_type=F32) + b2_ref[...]
    x1_buf[lax.rem(j, 2)] = xc_ref[0] + _rmsnorm(m, gpost_ref[...])

    _ffn_finish(j, x1_buf, act_ref, wd_ref, fpost_ref, o_ref)


def _conv_ffn(x, g_pre, w1_b, b1, wdw, bdw, ln_g, ln_b, w2_b, b2, g_post, ffn_params):
    b, s, _ = x.shape
    t = TOKEN_TILE
    per_seq, n_tiles, mixer, ffn = _tile_maps(b, s, t)
    per_halo = t // CONV_HALO
    n_halo = s // CONV_HALO

    def cur(j):
        bi, i = mixer(j)
        return bi, i, 0

    def prev(j):
        bi, i = mixer(j)
        return bi, jnp.maximum(i * per_halo - 1, 0), 0

    def nxt(j):
        bi, i = mixer(j)
        return bi, jnp.minimum((i + 1) * per_halo, n_halo - 1), 0

    halo = lambda index_map: pl.BlockSpec((1, CONV_HALO, D_MODEL), index_map)
    return pl.pallas_call(
        functools.partial(_conv_ffn_kernel, per_seq),
        grid=(n_tiles + 1,),
        in_specs=[halo(prev), pl.BlockSpec((1, t, D_MODEL), cur), halo(nxt),
                  _const_spec((1, D_MODEL)),
                  _const_spec((D_MODEL, 2 * D_MODEL)), _const_spec((1, 2 * D_MODEL)),
                  _const_spec((N_CHUNKS, CONV_WIDTH, LANES)), _const_spec((N_CHUNKS, 1, LANES)),
                  _const_spec((1, D_MODEL)), _const_spec((1, D_MODEL)),
                  _const_spec((D_MODEL, D_MODEL)), _const_spec((1, D_MODEL)),
                  _const_spec((1, D_MODEL)),
                  *_FFN_SPECS],
        out_specs=pl.BlockSpec((1, t, D_MODEL), ffn),
        out_shape=jax.ShapeDtypeStruct((b, s, D_MODEL), F32),
        scratch_shapes=[pltpu.VMEM((t + 2 * CONV_HALO, D_MODEL), BF16),
                        pltpu.VMEM((N_CHUNKS, t + 2 * CONV_HALO, LANES), F32),
                        pltpu.VMEM((N_CHUNKS, t, LANES), F32),
                        *_ffn_scratch(t)],
        compiler_params=_params(("arbitrary",)),
        name="conv_ffn",
    )(x, x, x, g_pre, w1_b, b1, wdw, bdw, ln_g, ln_b, w2_b, b2, g_post, *ffn_params)


def _rope_tables(seq_len):
    half = HEAD_DIM // 2
    inv_freq = ROPE_THETA ** (-jnp.arange(0, half, dtype=F32) * 2.0 / HEAD_DIM)
    ang = jnp.arange(seq_len).astype(F32)[:, None] * inv_freq[None, :]
    cos = jnp.cos(ang)
    sin = jnp.sin(ang)
    return jnp.tile(cos, (1, 4)), jnp.concatenate([-sin, sin, -sin, sin], axis=1)


def _chunk_major(w):
    rows = w.shape[0]
    return w.reshape(rows, N_CHUNKS, LANES).transpose(1, 0, 2)


def kernel(x_prompt, x_sample, mix_pre_g, mix_post_g, ffn_pre_g, ffn_post_g, w_in, attn_sink,
           w_pool, pool_scale, w_out, conv_w_pw1, conv_b_pw1, conv_w_dw, conv_b_dw, conv_ln_g,
           conv_ln_b, conv_w_pw2, conv_b_pw2, ffn_w_gate, ffn_w_up, ffn_w_down):
    depth = mix_pre_g.shape[0]
    row = lambda v: v.reshape(1, -1)
    col_scale = jnp.concatenate([jnp.full((ATTN_WIDTH,), HEAD_DIM ** -0.5, F32),
                                 jnp.ones((IN_PROJ_WIDTH - ATTN_WIDTH,), F32)])
    w_in_b = (w_in * col_scale).astype(BF16)
    w_pool_b = w_pool.astype(BF16)
    w_out_b = w_out.astype(BF16)
    w1_b = conv_w_pw1.astype(BF16)
    w2_b = conv_w_pw2.astype(BF16)
    wg_b = ffn_w_gate.astype(BF16)
    wu_b = ffn_w_up.astype(BF16)
    wd_b = ffn_w_down.astype(BF16)

    def trunk(x):
        _, s, _ = x.shape
        cos, sin = _rope_tables(s)
        for layer in range(depth):
            ffn_params = (row(ffn_pre_g[layer]), wg_b[layer], wu_b[layer], wd_b[layer],
                          row(ffn_post_g[layer]))
            if layer % 2 == 0:
                e = layer // 2
                q, k4, v4, u = _inproj(x, row(mix_pre_g[layer]), w_in_b[e], cos, sin)
                x = _attn_ffn(x, q, k4, v4, u, attn_sink[e], w_pool_b[e], row(pool_scale[e]),
                              w_out_b[e], row(mix_post_g[layer]), ffn_params)
            else:
                o = layer // 2
                x = _conv_ffn(x, row(mix_pre_g[layer]), w1_b[o], row(conv_b_pw1[o]),
                              _chunk_major(conv_w_dw[o]),
                              conv_b_dw[o].reshape(N_CHUNKS, 1, LANES),
                              row(conv_ln_g[o]), row(conv_ln_b[o]), w2_b[o],
                              row(conv_b_pw2[o]), row(mix_post_g[layer]), ffn_params)
        return x

    return (trunk(x_prompt), trunk(x_sample))
```

```python
import functools

import jax
import jax.numpy as jnp
from jax import lax
from jax.experimental import pallas as pl
from jax.experimental.pallas import tpu as pltpu

D_MODEL = 1024
HEAD_DIM = 64
N_Q_HEADS = 8
N_KV_HEADS = 2
ATTN_WIDTH = N_Q_HEADS * HEAD_DIM
KV_WIDTH = N_KV_HEADS * HEAD_DIM
WINDOW = 128
BLOCK = 128
BAND = 3 * BLOCK
ROPE_THETA = 10000.0
POOL_WINDOWS = (2, 4, 8, 16)
POOL_GROUP_DIM = 128
POOL_WIDTH = 512
POOL_HALO = 8
IN_PROJ_WIDTH = ATTN_WIDTH + 2 * KV_WIDTH + POOL_WIDTH
CONV_WIDTH = 31
CONV_HALO = 16
D_FF = 2816
EPS = 1e-6
NEG_INF = -1e30

LANES = 128
N_CHUNKS = D_MODEL // LANES
TOKEN_TILE = 512
WIDE_TILE = 1024
FF_CHUNK = 256
CONV_ROWS = 64
CONV_TAP_GROUP = 8
VMEM_LIMIT_BYTES = 56 * 1024 * 1024

F32 = jnp.float32
BF16 = jnp.bfloat16


def _rmsnorm(x, g):
    ms = jnp.mean(x * x, axis=-1, keepdims=True)
    return (x * lax.rsqrt(ms + EPS)) * g


def _params(n_axes):
    return pltpu.CompilerParams(
        dimension_semantics=("parallel",) * n_axes, vmem_limit_bytes=VMEM_LIMIT_BYTES)


def _const_spec(shape):
    zeros = (0,) * len(shape)
    return pl.BlockSpec(shape, lambda *_: zeros, pipeline_mode=pl.Buffered(1))


def _inproj_kernel(x_ref, g_ref, w_ref, cos_ref, sin_ref, q_ref, k4_ref, v4_ref, u_ref):
    t = x_ref.shape[1]
    h = _rmsnorm(x_ref[0], g_ref[...])
    z = jnp.dot(h.astype(BF16), w_ref[...], preferred_element_type=F32)
    cos = cos_ref[...]
    sin = sin_ref[...]
    lane = lax.broadcasted_iota(jnp.int32, (t, LANES), 1)
    first_half = (lane & (HEAD_DIM - 1)) < HEAD_DIM // 2
    low = lane < HEAD_DIM

    def rope(zc):
        swapped = jnp.where(first_half, pltpu.roll(zc, LANES - HEAD_DIM // 2, 1),
                            pltpu.roll(zc, HEAD_DIM // 2, 1))
        return zc * cos + swapped * sin

    def pad_heads(a):
        ar = pltpu.roll(a, HEAD_DIM, 1)
        zero = jnp.zeros_like(a)
        return (jnp.where(low, a, zero), jnp.where(low, zero, ar),
                jnp.where(low, ar, zero), jnp.where(low, zero, a))

    for c in range(ATTN_WIDTH // LANES):
        q_ref[0, :, c * LANES:(c + 1) * LANES] = rope(z[:, c * LANES:(c + 1) * LANES]).astype(BF16)
    k = rope(z[:, ATTN_WIDTH:ATTN_WIDTH + KV_WIDTH])
    v = z[:, ATTN_WIDTH + KV_WIDTH:ATTN_WIDTH + 2 * KV_WIDTH]
    for j, (kp, vp) in enumerate(zip(pad_heads(k), pad_heads(v))):
        k4_ref[0, :, j * LANES:(j + 1) * LANES] = kp.astype(BF16)
        v4_ref[0, :, j * LANES:(j + 1) * LANES] = vp.astype(BF16)
    u_ref[0] = z[:, ATTN_WIDTH + 2 * KV_WIDTH:]


def _inproj(x, g, w_in_b, cos, sin):
    b, s, _ = x.shape
    t = WIDE_TILE
    tok = lambda width: pl.BlockSpec((1, t, width), lambda bi, i: (bi, i, 0))
    table = pl.BlockSpec((t, LANES), lambda bi, i: (i, 0))
    return pl.pallas_call(
        _inproj_kernel,
        grid=(b, s // t),
        in_specs=[tok(D_MODEL), _const_spec((1, D_MODEL)), _const_spec((D_MODEL, IN_PROJ_WIDTH)),
                  table, table],
        out_specs=[tok(ATTN_WIDTH), tok(4 * LANES), tok(4 * LANES), tok(POOL_WIDTH)],
        out_shape=[jax.ShapeDtypeStruct((b, s, ATTN_WIDTH), BF16),
                   jax.ShapeDtypeStruct((b, s, 4 * LANES), BF16),
                   jax.ShapeDtypeStruct((b, s, 4 * LANES), BF16),
                   jax.ShapeDtypeStruct((b, s, POOL_WIDTH), F32)],
        compiler_params=_params(2),
        name="inproj_rope",
    )(x, g, w_in_b, cos, sin)


def _attn_pool_kernel(seq_len, sink_ref, x_ref, q_ref, kp_ref, kc_ref, kn_ref, vp_ref, vc_ref,
                      vn_ref, up_ref, uc_ref, un_ref, wpool_ref, pscale_ref, wout_ref, g_ref,
                      o_ref, kext, vext, uext, tri_ref, s_ref, p_ref, mix_ref):
    t = x_ref.shape[1]
    i = pl.program_id(1)
    n_tiles = pl.num_programs(1)
    n_blocks = t // BLOCK

    kext[0:BLOCK] = kp_ref[0]
    kext[BLOCK:BLOCK + t] = kc_ref[0]
    kext[BLOCK + t:] = kn_ref[0]
    vext[0:BLOCK] = vp_ref[0]
    vext[BLOCK:BLOCK + t] = vc_ref[0]
    vext[BLOCK + t:] = vn_ref[0]

    row = lax.broadcasted_iota(jnp.int32, (BLOCK, BLOCK), 0)
    col = lax.broadcasted_iota(jnp.int32, (BLOCK, BLOCK), 1)
    tri_ref[0] = jnp.where(col < row, 1.0, 0.0)
    tri_ref[1] = jnp.where(col > row, 1.0, 0.0)

    for jb in range(n_blocks):
        r0 = jb * BLOCK
        slot = jb % 2
        thr_lo = jnp.where(i == 0, -1.0, 0.5) if jb == 0 else 0.5
        thr_hi = jnp.where(i == n_tiles - 1, -1.0, 0.5) if jb == n_blocks - 1 else 0.5

        for ge in range(4):
            group = ge // 2
            q2 = jnp.concatenate(
                [q_ref[0, r0:r0 + BLOCK, (2 * group + cc) * LANES:(2 * group + cc + 1) * LANES]
                 for cc in range(2)], axis=0)
            kband = kext[r0:r0 + BAND, ge * LANES:(ge + 1) * LANES]
            s_ref[slot, ge] = lax.dot_general(q2, kband, (((1,), (1,)), ((), ())),
                                              preferred_element_type=F32)

        rinv = {}
        for ge in range(4):
            group, e = ge // 2, ge % 2
            for cc in range(2):
                rows = slice(cc * BLOCK, (cc + 1) * BLOCK)
                sink = sink_ref[2 * (2 * group + cc) + e]
                s0 = jnp.where(tri_ref[0] > thr_lo, NEG_INF, s_ref[slot, ge, rows, 0:BLOCK])
                s1 = s_ref[slot, ge, rows, BLOCK:2 * BLOCK]
                s2 = jnp.where(tri_ref[1] > thr_hi, NEG_INF, s_ref[slot, ge, rows, 2 * BLOCK:])
                m = jnp.max(jnp.maximum(jnp.maximum(s0, s1), s2), axis=-1, keepdims=True)
                m = jnp.maximum(m, sink)
                p0 = jnp.exp(s0 - m)
                p1 = jnp.exp(s1 - m)
                p2 = jnp.exp(s2 - m)
                denom = jnp.sum(p0 + p1 + p2, axis=-1, keepdims=True) + jnp.exp(sink - m)
                p_ref[slot, ge, rows, 0:BLOCK] = p0.astype(BF16)
                p_ref[slot, ge, rows, BLOCK:2 * BLOCK] = p1.astype(BF16)
                p_ref[slot, ge, rows, 2 * BLOCK:] = p2.astype(BF16)
                rinv[ge, cc] = 1.0 / denom

        for group in range(2):
            acc = [None, None]
            for e in range(2):
                ge = 2 * group + e
                vband = vext[r0:r0 + BAND, ge * LANES:(ge + 1) * LANES]
                o2 = jnp.dot(p_ref[slot, ge], vband, preferred_element_type=F32)
                for cc in range(2):
                    o = o2[cc * BLOCK:(cc + 1) * BLOCK] * rinv[ge, cc]
                    acc[cc] = o if acc[cc] is None else acc[cc] + o
            for cc in range(2):
                c = 2 * group + cc
                mix_ref[r0:r0 + BLOCK, c * LANES:(c + 1) * LANES] = acc[cc].astype(BF16)

    for gi, w in enumerate(POOL_WINDOWS):
        half = w // 2
        lanes = slice(gi * POOL_GROUP_DIM, (gi + 1) * POOL_GROUP_DIM)
        uext[gi, 0:POOL_HALO] = jnp.where(i > 0, up_ref[0, :, lanes], 0.0)
        uext[gi, POOL_HALO:POOL_HALO + t] = uc_ref[0, :, lanes]
        uext[gi, POOL_HALO + t:] = jnp.where(i < n_tiles - 1, un_ref[0, :, lanes], 0.0)
        wsum = uext[gi, POOL_HALO - half:POOL_HALO - half + t]
        for d in range(-half + 1, half):
            wsum = wsum + uext[gi, POOL_HALO + d:POOL_HALO + d + t]
        centre = uc_ref[0, :, lanes]

        def edge(r):
            pos = i * t + r + lax.broadcasted_iota(jnp.int32, (POOL_HALO, LANES), 0)
            cnt = jnp.minimum(pos + half, seq_len) - jnp.maximum(pos - half, 0)
            return wsum[r:r + POOL_HALO] / cnt.astype(F32) - centre[r:r + POOL_HALO]

        inner = slice(POOL_HALO, t - POOL_HALO)
        pooled = jnp.concatenate(
            [edge(0), wsum[inner] * (1.0 / w) - centre[inner], edge(t - POOL_HALO)], axis=0)
        mixed = jnp.dot(pooled.astype(BF16), wpool_ref[gi], preferred_element_type=F32)
        mixed = mixed * pscale_ref[:, lanes]
        mix_ref[:, ATTN_WIDTH + gi * POOL_GROUP_DIM:ATTN_WIDTH + (gi + 1) * POOL_GROUP_DIM] = (
            mixed.astype(BF16))

    m = jnp.dot(mix_ref[...], wout_ref[...], preferred_element_type=F32)
    o_ref[0] = x_ref[0] + _rmsnorm(m, g_ref[...])


def _attn_pool(x, q, k4, v4, u, sink, w_pool_b, pool_scale, w_out_b, g_post):
    b, s, _ = x.shape
    t = TOKEN_TILE
    per_blk = t // BLOCK
    n_blk = s // BLOCK
    per_halo = t // POOL_HALO
    n_halo = s // POOL_HALO
    tok = lambda width: pl.BlockSpec((1, t, width), lambda bi, i: (bi, i, 0))
    blk_prev = pl.BlockSpec((1, BLOCK, 4 * LANES),
                            lambda bi, i: (bi, jnp.maximum(i * per_blk - 1, 0), 0))
    blk_next = pl.BlockSpec((1, BLOCK, 4 * LANES),
                            lambda bi, i: (bi, jnp.minimum((i + 1) * per_blk, n_blk - 1), 0))
    halo_prev = pl.BlockSpec((1, POOL_HALO, POOL_WIDTH),
                             lambda bi, i: (bi, jnp.maximum(i * per_halo - 1, 0), 0))
    halo_next = pl.BlockSpec((1, POOL_HALO, POOL_WIDTH),
                             lambda bi, i: (bi, jnp.minimum((i + 1) * per_halo, n_halo - 1), 0))
    return pl.pallas_call(
        functools.partial(_attn_pool_kernel, s),
        grid=(b, s // t),
        in_specs=[pl.BlockSpec(memory_space=pltpu.SMEM),
                  tok(D_MODEL), tok(ATTN_WIDTH),
                  blk_prev, tok(4 * LANES), blk_next,
                  blk_prev, tok(4 * LANES), blk_next,
                  halo_prev, tok(POOL_WIDTH), halo_next,
                  _const_spec((len(POOL_WINDOWS), POOL_GROUP_DIM, POOL_GROUP_DIM)),
                  _const_spec((1, POOL_WIDTH)),
                  _const_spec((ATTN_WIDTH + POOL_WIDTH, D_MODEL)),
                  _const_spec((1, D_MODEL))],
        out_specs=tok(D_MODEL),
        out_shape=jax.ShapeDtypeStruct((b, s, D_MODEL), F32),
        scratch_shapes=[pltpu.VMEM((t + 2 * BLOCK, 4 * LANES), BF16),
                        pltpu.VMEM((t + 2 * BLOCK, 4 * LANES), BF16),
                        pltpu.VMEM((len(POOL_WINDOWS), t + 2 * POOL_HALO, LANES), F32),
                        pltpu.VMEM((2, BLOCK, BLOCK), F32),
                        pltpu.VMEM((2, 4, 2 * BLOCK, BAND), F32),
                        pltpu.VMEM((2, 4, 2 * BLOCK, BAND), BF16),
                        pltpu.VMEM((t, ATTN_WIDTH + POOL_WIDTH), BF16)],
        compiler_params=_params(2),
        name="attn_pool_outproj",
    )(sink, x, q, k4, k4, k4, v4, v4, v4, u, u, u, w_pool_b, pool_scale, w_out_b, g_post)


def _swiglu_kernel(x_ref, gpre_ref, wg_ref, wu_ref, wd_ref, gpost_ref, o_ref, act_ref):
    x = x_ref[...]
    hb = _rmsnorm(x, gpre_ref[...]).astype(BF16)
    for c in range(D_FF // FF_CHUNK):
        cols = slice(c * FF_CHUNK, (c + 1) * FF_CHUNK)
        gate = jnp.dot(hb, wg_ref[:, cols], preferred_element_type=F32)
        up = jnp.dot(hb, wu_ref[:, cols], preferred_element_type=F32)
        act = (gate * jax.nn.sigmoid(gate)) * up
        act_ref[:, cols] = act.astype(BF16)
    f = jnp.dot(act_ref[...], wd_ref[...], preferred_element_type=F32)
    o_ref[...] = x + _rmsnorm(f, gpost_ref[...])


def _swiglu(x2d, g_pre, wg_b, wu_b, wd_b, g_post):
    n, _ = x2d.shape
    t = WIDE_TILE
    tok = pl.BlockSpec((t, D_MODEL), lambda i: (i, 0))
    return pl.pallas_call(
        _swiglu_kernel,
        grid=(n // t,),
        in_specs=[tok, _const_spec((1, D_MODEL)),
                  _const_spec((D_MODEL, D_FF)), _const_spec((D_MODEL, D_FF)),
                  _const_spec((D_FF, D_MODEL)), _const_spec((1, D_MODEL))],
        out_specs=tok,
        out_shape=jax.ShapeDtypeStruct((n, D_MODEL), F32),
        scratch_shapes=[pltpu.VMEM((t, D_FF), BF16)],
        compiler_params=_params(1),
        name="swiglu",
    )(x2d, g_pre, wg_b, wu_b, wd_b, g_post)


def _tree_sum(terms):
    while len(terms) > 1:
        terms = [terms[n] + terms[n + 1] if n + 1 < len(terms) else terms[n]
                 for n in range(0, len(terms), 2)]
    return terms[0]


def _conv_kernel(xp_ref, xc_ref, xn_ref, gpre_ref, w1_ref, b1_ref, wdw_ref, bdw_ref, lng_ref,
                 lnb_ref, w2_ref, b2_ref, gpost_ref, o_ref, gch, gpk, ych):
    t = xc_ref.shape[1]
    i = pl.program_id(1)
    n_tiles = pl.num_programs(1)
    gpre = gpre_ref[...]
    x = xc_ref[0]
    hcat = jnp.concatenate([_rmsnorm(xp_ref[0], gpre).astype(BF16),
                            _rmsnorm(x, gpre).astype(BF16),
                            _rmsnorm(xn_ref[0], gpre).astype(BF16)], axis=0)
    z = jnp.dot(hcat, w1_ref[...], preferred_element_type=F32) + b1_ref[...]
    glu = z[:, :D_MODEL] * jax.nn.sigmoid(z[:, D_MODEL:])
    rowid = lax.broadcasted_iota(jnp.int32, (t + 2 * CONV_HALO, 1), 0)
    first_row = jnp.where(i > 0, 0, CONV_HALO)
    end_row = jnp.where(i < n_tiles - 1, t + 2 * CONV_HALO, t + CONV_HALO)
    glu = jnp.where((rowid >= first_row) & (rowid < end_row), glu, 0.0)
    for c in range(N_CHUNKS):
        gch[c] = glu[:, c * LANES:(c + 1) * LANES]

    n_pairs = gpk.shape[2]

    def chunk_body(c, carry):
        for parity in range(2):
            rows = gch[c, parity:parity + 2 * n_pairs, :].astype(BF16)
            gpk[parity, c] = pltpu.bitcast(rows, jnp.uint32)
        wk = wdw_ref[c].astype(BF16)
        bias = bdw_ref[c]
        for rb in range(t // CONV_ROWS):
            base = rb * CONV_ROWS + CONV_HALO - CONV_WIDTH // 2
            acc = jnp.broadcast_to(bias, (CONV_ROWS, LANES))
            for k0 in range(0, CONV_WIDTH, CONV_TAP_GROUP):
                prods = []
                for k in range(k0, min(k0 + CONV_TAP_GROUP, CONV_WIDTH)):
                    start = base + k
                    window = gpk[start % 2, c, start // 2:start // 2 + CONV_ROWS // 2, :]
                    prods.append(pltpu.bitcast(window, BF16) * wk[k:k + 1, :])
                acc = acc + _tree_sum(prods).astype(F32)
            ych[c, rb * CONV_ROWS:(rb + 1) * CONV_ROWS, :] = acc
        return carry

    lax.fori_loop(0, N_CHUNKS, chunk_body, 0)

    y = jnp.concatenate([ych[c] for c in range(N_CHUNKS)], axis=1)
    mu = jnp.mean(y, axis=-1, keepdims=True)
    var = jnp.mean(jnp.square(y - mu), axis=-1, keepdims=True)
    yn = (y - mu) * lax.rsqrt(var + EPS) * lng_ref[...] + lnb_ref[...]
    act = (yn * jax.nn.sigmoid(yn)).astype(BF16)
    m = jnp.dot(act, w2_ref[...], preferred_element_type=F32) + b2_ref[...]
    o_ref[0] = x + _rmsnorm(m, gpost_ref[...])


def _conv_module(x, g_pre, w1_b, b1, wdw, bdw, ln_g, ln_b, w2_b, b2, g_post):
    b, s, _ = x.shape
    t = TOKEN_TILE
    per_halo = t // CONV_HALO
    n_halo = s // CONV_HALO
    tok = pl.BlockSpec((1, t, D_MODEL), lambda bi, i: (bi, i, 0))
    halo_prev = pl.BlockSpec((1, CONV_HALO, D_MODEL),
                             lambda bi, i: (bi, jnp.maximum(i * per_halo - 1, 0), 0))
    halo_next = pl.BlockSpec((1, CONV_HALO, D_MODEL),
                             lambda bi, i: (bi, jnp.minimum((i + 1) * per_halo, n_halo - 1), 0))
    return pl.pallas_call(
        _conv_kernel,
        grid=(b, s // t),
        in_specs=[halo_prev, tok, halo_next,
                  _const_spec((1, D_MODEL)),
                  _const_spec((D_MODEL, 2 * D_MODEL)), _const_spec((1, 2 * D_MODEL)),
                  _const_spec((N_CHUNKS, CONV_WIDTH, LANES)), _const_spec((N_CHUNKS, 1, LANES)),
                  _const_spec((1, D_MODEL)), _const_spec((1, D_MODEL)),
                  _const_spec((D_MODEL, D_MODEL)), _const_spec((1, D_MODEL)),
                  _const_spec((1, D_MODEL))],
        out_specs=tok,
        out_shape=jax.ShapeDtypeStruct((b, s, D_MODEL), F32),
        scratch_shapes=[pltpu.VMEM((N_CHUNKS, t + 2 * CONV_HALO, LANES), F32),
                        pltpu.VMEM((2, N_CHUNKS, (t + 2 * CONV_HALO) // 2 - 1, LANES), jnp.uint32),
                        pltpu.VMEM((N_CHUNKS, t, LANES), F32)],
        compiler_params=_params(2),
        name="conformer_conv",
    )(x, x, x, g_pre, w1_b, b1, wdw, bdw, ln_g, ln_b, w2_b, b2, g_post)


def _rope_tables(seq_len):
    half = HEAD_DIM // 2
    inv_freq = ROPE_THETA ** (-jnp.arange(0, half, dtype=F32) * 2.0 / HEAD_DIM)
    ang = jnp.arange(seq_len).astype(F32)[:, None] * inv_freq[None, :]
    cos = jnp.cos(ang)
    sin = jnp.sin(ang)
    return jnp.tile(cos, (1, 4)), jnp.concatenate([-sin, sin, -sin, sin], axis=1)


def _chunk_major(w):
    rows = w.shape[0]
    return w.reshape(rows, N_CHUNKS, LANES).transpose(1, 0, 2)


def kernel(x_prompt, x_sample, mix_pre_g, mix_post_g, ffn_pre_g, ffn_post_g, w_in, attn_sink,
           w_pool, pool_scale, w_out, conv_w_pw1, conv_b_pw1, conv_w_dw, conv_b_dw, conv_ln_g,
           conv_ln_b, conv_w_pw2, conv_b_pw2, ffn_w_gate, ffn_w_up, ffn_w_down):
    depth = mix_pre_g.shape[0]
    row = lambda v: v.reshape(1, -1)
    col_scale = jnp.concatenate([jnp.full((ATTN_WIDTH,), HEAD_DIM ** -0.5, F32),
                                 jnp.ones((IN_PROJ_WIDTH - ATTN_WIDTH,), F32)])
    w_in_b = (w_in * col_scale).astype(BF16)
    w_pool_b = w_pool.astype(BF16)
    w_out_b = w_out.astype(BF16)
    w1_b = conv_w_pw1.astype(BF16)
    w2_b = conv_w_pw2.astype(BF16)
    wg_b = ffn_w_gate.astype(BF16)
    wu_b = ffn_w_up.astype(BF16)
    wd_b = ffn_w_down.astype(BF16)

    def trunk(x):
        b, s, _ = x.shape
        cos, sin = _rope_tables(s)
        for layer in range(depth):
            if layer % 2 == 0:
                e = layer // 2
                q, k4, v4, u = _inproj(x, row(mix_pre_g[layer]), w_in_b[e], cos, sin)
                x = _attn_pool(x, q, k4, v4, u, attn_sink[e], w_pool_b[e], row(pool_scale[e]),
                               w_out_b[e], row(mix_post_g[layer]))
            else:
                o = layer // 2
                x = _conv_module(x, row(mix_pre_g[layer]), w1_b[o], row(conv_b_pw1[o]),
                                 _chunk_major(conv_w_dw[o]),
                                 conv_b_dw[o].reshape(N_CHUNKS, 1, LANES),
                                 row(conv_ln_g[o]), row(conv_ln_b[o]), w2_b[o],
                                 row(conv_b_pw2[o]), row(mix_post_g[layer]))
            x = _swiglu(x.reshape(b * s, D_MODEL), row(ffn_pre_g[layer]), wg_b[layer],
                        wu_b[layer], wd_b[layer], row(ffn_post_g[layer])).reshape(b, s, D_MODEL)
        return x

    return (trunk(x_prompt), trunk(x_sample))
```

```python
import functools

import jax
import jax.numpy as jnp
from jax import lax
from jax.experimental import pallas as pl
from jax.experimental.pallas import tpu as pltpu

D_MODEL = 1024
HEAD_DIM = 64
N_Q_HEADS = 8
N_KV_HEADS = 2
ATTN_WIDTH = N_Q_HEADS * HEAD_DIM
KV_WIDTH = N_KV_HEADS * HEAD_DIM
WINDOW = 128
BLOCK = 128
BAND = 3 * BLOCK
ROPE_THETA = 10000.0
POOL_WINDOWS = (2, 4, 8, 16)
POOL_GROUP_DIM = 128
POOL_WIDTH = 512
POOL_HALO = 8
IN_PROJ_WIDTH = ATTN_WIDTH + 2 * KV_WIDTH + POOL_WIDTH
CONV_WIDTH = 31
CONV_HALO = 16
D_FF = 2816
EPS = 1e-6
NEG_INF = -1e30

LANES = 128
N_CHUNKS = D_MODEL // LANES
TOKEN_TILE = 512
WIDE_TILE = 1024
FF_CHUNK = 256
CONV_ROWS = 64
CONV_TAP_GROUP = 8
VMEM_LIMIT_BYTES = 56 * 1024 * 1024

F32 = jnp.float32
BF16 = jnp.bfloat16


def _rmsnorm(x, g):
    ms = jnp.mean(x * x, axis=-1, keepdims=True)
    return (x * lax.rsqrt(ms + EPS)) * g


def _params(n_axes):
    return pltpu.CompilerParams(
        dimension_semantics=("parallel",) * n_axes, vmem_limit_bytes=VMEM_LIMIT_BYTES)


def _row_halves(t):
    return (slice(0, t // 2), slice(t // 2, t))


def _const_spec(shape):
    zeros = (0,) * len(shape)
    return pl.BlockSpec(shape, lambda *_: zeros, pipeline_mode=pl.Buffered(1))


def _inproj_kernel(x_ref, g_ref, w_ref, cos_ref, sin_ref, q_ref, k4_ref, v4_ref, u_ref):
    t = x_ref.shape[1]
    h = _rmsnorm(x_ref[0], g_ref[...])
    z = jnp.dot(h.astype(BF16), w_ref[...], preferred_element_type=F32)
    cos = cos_ref[...]
    sin = sin_ref[...]
    lane = lax.broadcasted_iota(jnp.int32, (t, LANES), 1)
    first_half = (lane & (HEAD_DIM - 1)) < HEAD_DIM // 2
    low = lane < HEAD_DIM

    def rope(zc):
        swapped = jnp.where(first_half, pltpu.roll(zc, LANES - HEAD_DIM // 2, 1),
                            pltpu.roll(zc, HEAD_DIM // 2, 1))
        return zc * cos + swapped * sin

    def pad_heads(a):
        ar = pltpu.roll(a, HEAD_DIM, 1)
        zero = jnp.zeros_like(a)
        return (jnp.where(low, a, zero), jnp.where(low, zero, ar),
                jnp.where(low, ar, zero), jnp.where(low, zero, a))

    for c in range(ATTN_WIDTH // LANES):
        q_ref[0, :, c * LANES:(c + 1) * LANES] = rope(z[:, c * LANES:(c + 1) * LANES]).astype(BF16)
    k = rope(z[:, ATTN_WIDTH:ATTN_WIDTH + KV_WIDTH])
    v = z[:, ATTN_WIDTH + KV_WIDTH:ATTN_WIDTH + 2 * KV_WIDTH]
    for j, (kp, vp) in enumerate(zip(pad_heads(k), pad_heads(v))):
        k4_ref[0, :, j * LANES:(j + 1) * LANES] = kp.astype(BF16)
        v4_ref[0, :, j * LANES:(j + 1) * LANES] = vp.astype(BF16)
    u_ref[0] = z[:, ATTN_WIDTH + 2 * KV_WIDTH:]


def _inproj(x, g, w_in_b, cos, sin):
    b, s, _ = x.shape
    t = WIDE_TILE
    tok = lambda width: pl.BlockSpec((1, t, width), lambda bi, i: (bi, i, 0))
    table = pl.BlockSpec((t, LANES), lambda bi, i: (i, 0))
    return pl.pallas_call(
        _inproj_kernel,
        grid=(b, s // t),
        in_specs=[tok(D_MODEL), _const_spec((1, D_MODEL)), _const_spec((D_MODEL, IN_PROJ_WIDTH)),
                  table, table],
        out_specs=[tok(ATTN_WIDTH), tok(4 * LANES), tok(4 * LANES), tok(POOL_WIDTH)],
        out_shape=[jax.ShapeDtypeStruct((b, s, ATTN_WIDTH), BF16),
                   jax.ShapeDtypeStruct((b, s, 4 * LANES), BF16),
                   jax.ShapeDtypeStruct((b, s, 4 * LANES), BF16),
                   jax.ShapeDtypeStruct((b, s, POOL_WIDTH), F32)],
        compiler_params=_params(2),
        name="inproj_rope",
    )(x, g, w_in_b, cos, sin)


def _attn_pool_kernel(seq_len, sink_ref, x_ref, q_ref, kp_ref, kc_ref, kn_ref, vp_ref, vc_ref,
                      vn_ref, up_ref, uc_ref, un_ref, wpool_ref, pscale_ref, wout_ref, g_ref,
                      o_ref, kext, vext, uext, cap_ref, s_ref, p_ref, mix_ref):
    t = x_ref.shape[1]
    i = pl.program_id(1)
    n_tiles = pl.num_programs(1)
    n_blocks = t // BLOCK

    kext[0:BLOCK] = kp_ref[0]
    kext[BLOCK:BLOCK + t] = kc_ref[0]
    kext[BLOCK + t:] = kn_ref[0]
    vext[0:BLOCK] = vp_ref[0]
    vext[BLOCK:BLOCK + t] = vc_ref[0]
    vext[BLOCK + t:] = vn_ref[0]

    row = lax.broadcasted_iota(jnp.int32, (BLOCK, BLOCK), 0)
    col = lax.broadcasted_iota(jnp.int32, (BLOCK, BLOCK), 1)
    no_cap = float(jnp.finfo(F32).max)
    cap_ref[1] = jnp.where(col < row, NEG_INF, no_cap)
    cap_ref[2] = jnp.where(col > row, NEG_INF, no_cap)
    cap_ref[0] = jnp.where(col < jnp.maximum(row, jnp.where(i == 0, BLOCK, 0)), NEG_INF, no_cap)
    cap_ref[3] = jnp.where(col > jnp.minimum(row, jnp.where(i == n_tiles - 1, -1, BLOCK)),
                           NEG_INF, no_cap)

    for gi, w in enumerate(POOL_WINDOWS):
        half = w // 2
        lanes = slice(gi * POOL_GROUP_DIM, (gi + 1) * POOL_GROUP_DIM)
        uext[gi, 0:POOL_HALO] = jnp.where(i > 0, up_ref[0, :, lanes], 0.0)
        uext[gi, POOL_HALO:POOL_HALO + t] = uc_ref[0, :, lanes]
        uext[gi, POOL_HALO + t:] = jnp.where(i < n_tiles - 1, un_ref[0, :, lanes], 0.0)
        wsum = uext[gi, POOL_HALO - half:POOL_HALO - half + t]
        for d in range(-half + 1, half):
            wsum = wsum + uext[gi, POOL_HALO + d:POOL_HALO + d + t]
        centre = uc_ref[0, :, lanes]

        def edge(r):
            pos = i * t + r + lax.broadcasted_iota(jnp.int32, (POOL_HALO, LANES), 0)
            cnt = jnp.minimum(pos + half, seq_len) - jnp.maximum(pos - half, 0)
            return wsum[r:r + POOL_HALO] / cnt.astype(F32) - centre[r:r + POOL_HALO]

        inner = slice(POOL_HALO, t - POOL_HALO)
        pooled = jnp.concatenate(
            [edge(0), wsum[inner] * (1.0 / w) - centre[inner], edge(t - POOL_HALO)], axis=0)
        mixed = jnp.dot(pooled.astype(BF16), wpool_ref[gi], preferred_element_type=F32)
        mixed = mixed * pscale_ref[:, lanes]
        mix_ref[:, ATTN_WIDTH + gi * POOL_GROUP_DIM:ATTN_WIDTH + (gi + 1) * POOL_GROUP_DIM] = (
            mixed.astype(BF16))

    for jb in range(n_blocks):
        r0 = jb * BLOCK
        slot = jb % 2
        cap_lo = 0 if jb == 0 else 1
        cap_hi = 3 if jb == n_blocks - 1 else 2

        for ge in range(4):
            group = ge // 2
            q2 = jnp.concatenate(
                [q_ref[0, r0:r0 + BLOCK, (2 * group + cc) * LANES:(2 * group + cc + 1) * LANES]
                 for cc in range(2)], axis=0)
            kband = kext[r0:r0 + BAND, ge * LANES:(ge + 1) * LANES]
            s_ref[slot, ge] = lax.dot_general(q2, kband, (((1,), (1,)), ((), ())),
                                              preferred_element_type=F32)

        rinv = {}
        for ge in range(4):
            group, e = ge // 2, ge % 2
            for cc in range(2):
                rows = slice(cc * BLOCK, (cc + 1) * BLOCK)
                sink = sink_ref[2 * (2 * group + cc) + e]
                s0 = jnp.minimum(s_ref[slot, ge, rows, 0:BLOCK], cap_ref[cap_lo])
                s1 = s_ref[slot, ge, rows, BLOCK:2 * BLOCK]
                s2 = jnp.minimum(s_ref[slot, ge, rows, 2 * BLOCK:], cap_ref[cap_hi])
                m = jnp.max(jnp.maximum(jnp.maximum(s0, s1), s2), axis=-1, keepdims=True)
                m = jnp.maximum(m, sink)
                p0 = jnp.exp(s0 - m)
                p1 = jnp.exp(s1 - m)
                p2 = jnp.exp(s2 - m)
                denom = jnp.sum(p0 + p1 + p2, axis=-1, keepdims=True) + jnp.exp(sink - m)
                p_ref[slot, ge, rows, 0:BLOCK] = p0.astype(BF16)
                p_ref[slot, ge, rows, BLOCK:2 * BLOCK] = p1.astype(BF16)
                p_ref[slot, ge, rows, 2 * BLOCK:] = p2.astype(BF16)
                rinv[ge, cc] = 1.0 / denom

        for group in range(2):
            acc = [None, None]
            for e in range(2):
                ge = 2 * group + e
                vband = vext[r0:r0 + BAND, ge * LANES:(ge + 1) * LANES]
                o2 = jnp.dot(p_ref[slot, ge], vband, preferred_element_type=F32)
                for cc in range(2):
                    o = o2[cc * BLOCK:(cc + 1) * BLOCK] * rinv[ge, cc]
                    acc[cc] = o if acc[cc] is None else acc[cc] + o
            for cc in range(2):
                c = 2 * group + cc
                mix_ref[r0:r0 + BLOCK, c * LANES:(c + 1) * LANES] = acc[cc].astype(BF16)

    for rows in _row_halves(t):
        m = jnp.dot(mix_ref[rows, :], wout_ref[...], preferred_element_type=F32)
        o_ref[0, rows, :] = x_ref[0, rows, :] + _rmsnorm(m, g_ref[...])


def _attn_pool(x, q, k4, v4, u, sink, w_pool_b, pool_scale, w_out_b, g_post):
    b, s, _ = x.shape
    t = TOKEN_TILE
    per_blk = t // BLOCK
    n_blk = s // BLOCK
    per_halo = t // POOL_HALO
    n_halo = s // POOL_HALO
    tok = lambda width: pl.BlockSpec((1, t, width), lambda bi, i: (bi, i, 0))
    blk_prev = pl.BlockSpec((1, BLOCK, 4 * LANES),
                            lambda bi, i: (bi, jnp.maximum(i * per_blk - 1, 0), 0))
    blk_next = pl.BlockSpec((1, BLOCK, 4 * LANES),
                            lambda bi, i: (bi, jnp.minimum((i + 1) * per_blk, n_blk - 1), 0))
    halo_prev = pl.BlockSpec((1, POOL_HALO, POOL_WIDTH),
                             lambda bi, i: (bi, jnp.maximum(i * per_halo - 1, 0), 0))
    halo_next = pl.BlockSpec((1, POOL_HALO, POOL_WIDTH),
                             lambda bi, i: (bi, jnp.minimum((i + 1) * per_halo, n_halo - 1), 0))
    return pl.pallas_call(
        functools.partial(_attn_pool_kernel, s),
        grid=(b, s // t),
        in_specs=[pl.BlockSpec(memory_space=pltpu.SMEM),
                  tok(D_MODEL), tok(ATTN_WIDTH),
                  blk_prev, tok(4 * LANES), blk_next,
                  blk_prev, tok(4 * LANES), blk_next,
                  halo_prev, tok(POOL_WIDTH), halo_next,
                  _const_spec((len(POOL_WINDOWS), POOL_GROUP_DIM, POOL_GROUP_DIM)),
                  _const_spec((1, POOL_WIDTH)),
                  _const_spec((ATTN_WIDTH + POOL_WIDTH, D_MODEL)),
                  _const_spec((1, D_MODEL))],
        out_specs=tok(D_MODEL),
        out_shape=jax.ShapeDtypeStruct((b, s, D_MODEL), F32),
        scratch_shapes=[pltpu.VMEM((t + 2 * BLOCK, 4 * LANES), BF16),
                        pltpu.VMEM((t + 2 * BLOCK, 4 * LANES), BF16),
                        pltpu.VMEM((len(POOL_WINDOWS), t + 2 * POOL_HALO, LANES), F32),
                        pltpu.VMEM((4, BLOCK, BLOCK), F32),
                        pltpu.VMEM((2, 4, 2 * BLOCK, BAND), F32),
                        pltpu.VMEM((2, 4, 2 * BLOCK, BAND), BF16),
                        pltpu.VMEM((t, ATTN_WIDTH + POOL_WIDTH), BF16)],
        compiler_params=_params(2),
        name="attn_pool_outproj",
    )(sink, x, q, k4, k4, k4, v4, v4, v4, u, u, u, w_pool_b, pool_scale, w_out_b, g_post)


def _swiglu_kernel(x_ref, gpre_ref, wg_ref, wu_ref, wd_ref, gpost_ref, o_ref, act_ref):
    for rows in _row_halves(x_ref.shape[0]):
        hb = _rmsnorm(x_ref[rows, :], gpre_ref[...]).astype(BF16)
        for c in range(D_FF // FF_CHUNK):
            cols = slice(c * FF_CHUNK, (c + 1) * FF_CHUNK)
            gate = jnp.dot(hb, wg_ref[:, cols], preferred_element_type=F32)
            up = jnp.dot(hb, wu_ref[:, cols], preferred_element_type=F32)
            act = (gate * jax.nn.sigmoid(gate)) * up
            act_ref[rows, cols] = act.astype(BF16)
        f = jnp.dot(act_ref[rows, :], wd_ref[...], preferred_element_type=F32)
        o_ref[rows, :] = x_ref[rows, :] + _rmsnorm(f, gpost_ref[...])


def _swiglu(x2d, g_pre, wg_b, wu_b, wd_b, g_post, layer):
    n, _ = x2d.shape
    t = WIDE_TILE
    tok = pl.BlockSpec((t, D_MODEL), lambda i: (i, 0))
    layer_spec = lambda rows, cols: pl.BlockSpec((None, rows, cols), lambda i: (layer, 0, 0),
                                                 pipeline_mode=pl.Buffered(1))
    return pl.pallas_call(
        _swiglu_kernel,
        grid=(n // t,),
        in_specs=[tok, _const_spec((1, D_MODEL)),
                  layer_spec(D_MODEL, D_FF), layer_spec(D_MODEL, D_FF),
                  layer_spec(D_FF, D_MODEL), _const_spec((1, D_MODEL))],
        out_specs=tok,
        out_shape=jax.ShapeDtypeStruct((n, D_MODEL), F32),
        scratch_shapes=[pltpu.VMEM((t, D_FF), BF16)],
        compiler_params=_params(1),
        name="swiglu",
    )(x2d, g_pre, wg_b, wu_b, wd_b, g_post)


def _tree_sum(terms):
    while len(terms) > 1:
        terms = [terms[n] + terms[n + 1] if n + 1 < len(terms) else terms[n]
                 for n in range(0, len(terms), 2)]
    return terms[0]


def _conv_kernel(xp_ref, xc_ref, xn_ref, gpre_ref, w1_ref, b1_ref, wdw_ref, bdw_ref, lng_ref,
                 lnb_ref, w2_ref, b2_ref, gpost_ref, o_ref, gch, gpk, ych):
    t = xc_ref.shape[1]
    i = pl.program_id(1)
    n_tiles = pl.num_programs(1)
    gpre = gpre_ref[...]
    x = xc_ref[0]
    hcat = jnp.concatenate([_rmsnorm(xp_ref[0], gpre).astype(BF16),
                            _rmsnorm(x, gpre).astype(BF16),
                            _rmsnorm(xn_ref[0], gpre).astype(BF16)], axis=0)
    z = jnp.dot(hcat, w1_ref[...], preferred_element_type=F32) + b1_ref[...]
    glu = z[:, :D_MODEL] * jax.nn.sigmoid(z[:, D_MODEL:])
    rowid = lax.broadcasted_iota(jnp.int32, (t + 2 * CONV_HALO, 1), 0)
    first_row = jnp.where(i > 0, 0, CONV_HALO)
    end_row = jnp.where(i < n_tiles - 1, t + 2 * CONV_HALO, t + CONV_HALO)
    glu = jnp.where((rowid >= first_row) & (rowid < end_row), glu, 0.0)
    for c in range(N_CHUNKS):
        gch[c] = glu[:, c * LANES:(c + 1) * LANES]

    n_pairs = gpk.shape[2]

    def chunk_body(c, carry):
        for parity in range(2):
            rows = gch[c, parity:parity + 2 * n_pairs, :].astype(BF16)
            gpk[parity, c] = pltpu.bitcast(rows, jnp.uint32)
        wk = wdw_ref[c].astype(BF16)
        bias = bdw_ref[c]
        for rb in range(t // CONV_ROWS):
            base = rb * CONV_ROWS + CONV_HALO - CONV_WIDTH // 2
            acc = jnp.broadcast_to(bias, (CONV_ROWS, LANES))
            for k0 in range(0, CONV_WIDTH, CONV_TAP_GROUP):
                prods = []
                for k in range(k0, min(k0 + CONV_TAP_GROUP, CONV_WIDTH)):
                    start = base + k
                    window = gpk[start % 2, c, start // 2:start // 2 + CONV_ROWS // 2, :]
                    prods.append(pltpu.bitcast(window, BF16) * wk[k:k + 1, :])
                acc = acc + _tree_sum(prods).astype(F32)
            ych[c, rb * CONV_ROWS:(rb + 1) * CONV_ROWS, :] = acc
        return carry

    lax.fori_loop(0, N_CHUNKS, chunk_body, 0)

    for rows in _row_halves(t):
        y = jnp.concatenate([ych[c, rows, :] for c in range(N_CHUNKS)], axis=1)
        mu = jnp.mean(y, axis=-1, keepdims=True)
        var = jnp.mean(jnp.square(y - mu), axis=-1, keepdims=True)
        yn = (y - mu) * lax.rsqrt(var + EPS) * lng_ref[...] + lnb_ref[...]
        act = (yn * jax.nn.sigmoid(yn)).astype(BF16)
        m = jnp.dot(act, w2_ref[...], preferred_element_type=F32) + b2_ref[...]
        o_ref[0, rows, :] = xc_ref[0, rows, :] + _rmsnorm(m, gpost_ref[...])


def _conv_module(x, g_pre, w1_b, b1, wdw, bdw, ln_g, ln_b, w2_b, b2, g_post):
    b, s, _ = x.shape
    t = TOKEN_TILE
    per_halo = t // CONV_HALO
    n_halo = s // CONV_HALO
    tok = pl.BlockSpec((1, t, D_MODEL), lambda bi, i: (bi, i, 0))
    halo_prev = pl.BlockSpec((1, CONV_HALO, D_MODEL),
                             lambda bi, i: (bi, jnp.maximum(i * per_halo - 1, 0), 0))
    halo_next = pl.BlockSpec((1, CONV_HALO, D_MODEL),
                             lambda bi, i: (bi, jnp.minimum((i + 1) * per_halo, n_halo - 1), 0))
    return pl.pallas_call(
        _conv_kernel,
        grid=(b, s // t),
        in_specs=[halo_prev, tok, halo_next,
                  _const_spec((1, D_MODEL)),
                  _const_spec((D_MODEL, 2 * D_MODEL)), _const_spec((1, 2 * D_MODEL)),
                  _const_spec((N_CHUNKS, CONV_WIDTH, LANES)), _const_spec((N_CHUNKS, 1, LANES)),
                  _const_spec((1, D_MODEL)), _const_spec((1, D_MODEL)),
                  _const_spec((D_MODEL, D_MODEL)), _const_spec((1, D_MODEL)),
                  _const_spec((1, D_MODEL))],
        out_specs=tok,
        out_shape=jax.ShapeDtypeStruct((b, s, D_MODEL), F32),
        scratch_shapes=[pltpu.VMEM((N_CHUNKS, t + 2 * CONV_HALO, LANES), F32),
                        pltpu.VMEM((2, N_CHUNKS, (t + 2 * CONV_HALO) // 2 - 1, LANES), jnp.uint32),
                        pltpu.VMEM((N_CHUNKS, t, LANES), F32)],
        compiler_params=_params(2),
        name="conformer_conv",
    )(x, x, x, g_pre, w1_b, b1, wdw, bdw, ln_g, ln_b, w2_b, b2, g_post)


def _rope_tables(seq_len):
    half = HEAD_DIM // 2
    inv_freq = ROPE_THETA ** (-jnp.arange(0, half, dtype=F32) * 2.0 / HEAD_DIM)
    ang = jnp.arange(seq_len).astype(F32)[:, None] * inv_freq[None, :]
    cos = jnp.cos(ang)
    sin = jnp.sin(ang)
    return jnp.tile(cos, (1, 4)), jnp.concatenate([-sin, sin, -sin, sin], axis=1)


def _chunk_major(w):
    rows = w.shape[0]
    return w.reshape(rows, N_CHUNKS, LANES).transpose(1, 0, 2)


def kernel(x_prompt, x_sample, mix_pre_g, mix_post_g, ffn_pre_g, ffn_post_g, w_in, attn_sink,
           w_pool, pool_scale, w_out, conv_w_pw1, conv_b_pw1, conv_w_dw, conv_b_dw, conv_ln_g,
           conv_ln_b, conv_w_pw2, conv_b_pw2, ffn_w_gate, ffn_w_up, ffn_w_down):
    depth = mix_pre_g.shape[0]
    row = lambda v: v.reshape(1, -1)
    col_scale = jnp.concatenate([jnp.full((ATTN_WIDTH,), HEAD_DIM ** -0.5, F32),
                                 jnp.ones((IN_PROJ_WIDTH - ATTN_WIDTH,), F32)])
    w_in_b = (w_in * col_scale).astype(BF16)
    w_pool_b = w_pool.astype(BF16)
    w_out_b = w_out.astype(BF16)
    w1_b = conv_w_pw1.astype(BF16)
    w2_b = conv_w_pw2.astype(BF16)
    wg_b = ffn_w_gate.astype(BF16)
    wu_b = ffn_w_up.astype(BF16)
    wd_b = ffn_w_down.astype(BF16)

    def trunk(x):
        b, s, _ = x.shape
        cos, sin = _rope_tables(s)
        for layer in range(depth):
            if layer % 2 == 0:
                e = layer // 2
                q, k4, v4, u = _inproj(x, row(mix_pre_g[layer]), w_in_b[e], cos, sin)
                x = _attn_pool(x, q, k4, v4, u, attn_sink[e], w_pool_b[e], row(pool_scale[e]),
                               w_out_b[e], row(mix_post_g[layer]))
            else:
                o = layer // 2
                x = _conv_module(x, row(mix_pre_g[layer]), w1_b[o], row(conv_b_pw1[o]),
                                 _chunk_major(conv_w_dw[o]),
                                 conv_b_dw[o].reshape(N_CHUNKS, 1, LANES),
                                 row(conv_ln_g[o]), row(conv_ln_b[o]), w2_b[o],
                                 row(conv_b_pw2[o]), row(mix_post_g[layer]))
            x = _swiglu(x.reshape(b * s, D_MODEL), row(ffn_pre_g[layer]), wg_b, wu_b, wd_b,
                        row(ffn_post_g[layer]), layer).reshape(b, s, D_MODEL)
        return x

    return (trunk(x_prompt), trunk(x_sample))
```

```python
import functools

import jax
import jax.numpy as jnp
from jax import lax
from jax.experimental import pallas as pl
from jax.experimental.pallas import tpu as pltpu

D_MODEL = 1024
HEAD_DIM = 64
N_Q_HEADS = 8
N_KV_HEADS = 2
ATTN_WIDTH = N_Q_HEADS * HEAD_DIM
KV_WIDTH = N_KV_HEADS * HEAD_DIM
WINDOW = 128
BLOCK = 128
BAND = 3 * BLOCK
ROPE_THETA = 10000.0
POOL_WINDOWS = (2, 4, 8, 16)
POOL_GROUP_DIM = 128
POOL_WIDTH = 512
POOL_HALO = 8
IN_PROJ_WIDTH = ATTN_WIDTH + 2 * KV_WIDTH + POOL_WIDTH
CONV_WIDTH = 31
CONV_HALO = 16
D_FF = 2816
EPS = 1e-6
NEG_INF = -1e30

LANES = 128
N_CHUNKS = D_MODEL // LANES
TOKEN_TILE = 512
WIDE_TILE = 1024
FF_CHUNK = 256
CONV_ROWS = 64
CONV_TAP_GROUP = 8
VMEM_LIMIT_BYTES = 56 * 1024 * 1024

F32 = jnp.float32
BF16 = jnp.bfloat16


def _rmsnorm(x, g):
    ms = jnp.mean(x * x, axis=-1, keepdims=True)
    return (x * lax.rsqrt(ms + EPS)) * g


def _params(n_axes):
    return pltpu.CompilerParams(
        dimension_semantics=("parallel",) * n_axes, vmem_limit_bytes=VMEM_LIMIT_BYTES)


def _row_halves(t):
    return (slice(0, t // 2), slice(t // 2, t))


def _const_spec(shape):
    zeros = (0,) * len(shape)
    return pl.BlockSpec(shape, lambda *_: zeros, pipeline_mode=pl.Buffered(1))


def _inproj_kernel(x_ref, g_ref, w_ref, cos_ref, sin_ref, q_ref, k4_ref, v4_ref, u_ref):
    t = x_ref.shape[1]
    h = _rmsnorm(x_ref[0], g_ref[...])
    z = jnp.dot(h.astype(BF16), w_ref[...], preferred_element_type=F32)
    cos = cos_ref[...]
    sin = sin_ref[...]
    lane = lax.broadcasted_iota(jnp.int32, (t, LANES), 1)
    first_half = (lane & (HEAD_DIM - 1)) < HEAD_DIM // 2
    low = lane < HEAD_DIM

    def rope(zc):
        swapped = jnp.where(first_half, pltpu.roll(zc, LANES - HEAD_DIM // 2, 1),
                            pltpu.roll(zc, HEAD_DIM // 2, 1))
        return zc * cos + swapped * sin

    def pad_heads(a):
        ar = pltpu.roll(a, HEAD_DIM, 1)
        zero = jnp.zeros_like(a)
        return (jnp.where(low, a, zero), jnp.where(low, zero, ar),
                jnp.where(low, ar, zero), jnp.where(low, zero, a))

    for c in range(ATTN_WIDTH // LANES):
        q_ref[0, :, c * LANES:(c + 1) * LANES] = rope(z[:, c * LANES:(c + 1) * LANES]).astype(BF16)
    k = rope(z[:, ATTN_WIDTH:ATTN_WIDTH + KV_WIDTH])
    v = z[:, ATTN_WIDTH + KV_WIDTH:ATTN_WIDTH + 2 * KV_WIDTH]
    for j, (kp, vp) in enumerate(zip(pad_heads(k), pad_heads(v))):
        k4_ref[0, :, j * LANES:(j + 1) * LANES] = kp.astype(BF16)
        v4_ref[0, :, j * LANES:(j + 1) * LANES] = vp.astype(BF16)
    u_ref[0] = z[:, ATTN_WIDTH + 2 * KV_WIDTH:]


def _inproj(x, g, w_in_b, cos, sin):
    b, s, _ = x.shape
    t = WIDE_TILE
    tok = lambda width: pl.BlockSpec((1, t, width), lambda bi, i: (bi, i, 0))
    table = pl.BlockSpec((t, LANES), lambda bi, i: (i, 0))
    return pl.pallas_call(
        _inproj_kernel,
        grid=(b, s // t),
        in_specs=[tok(D_MODEL), _const_spec((1, D_MODEL)), _const_spec((D_MODEL, IN_PROJ_WIDTH)),
                  table, table],
        out_specs=[tok(ATTN_WIDTH), tok(4 * LANES), tok(4 * LANES), tok(POOL_WIDTH)],
        out_shape=[jax.ShapeDtypeStruct((b, s, ATTN_WIDTH), BF16),
                   jax.ShapeDtypeStruct((b, s, 4 * LANES), BF16),
                   jax.ShapeDtypeStruct((b, s, 4 * LANES), BF16),
                   jax.ShapeDtypeStruct((b, s, POOL_WIDTH), F32)],
        compiler_params=_params(2),
        name="inproj_rope",
    )(x, g, w_in_b, cos, sin)


def _attn_pool_kernel(seq_len, sink_ref, x_ref, q_ref, kp_ref, kc_ref, kn_ref, vp_ref, vc_ref,
                      vn_ref, up_ref, uc_ref, un_ref, wpool_ref, pscale_ref, wout_ref, g_ref,
                      o_ref, kext, vext, uext, cap_ref, s_ref, p_ref, mix_ref):
    t = x_ref.shape[1]
    i = pl.program_id(1)
    n_tiles = pl.num_programs(1)
    n_blocks = t // BLOCK

    kext[0:BLOCK] = kp_ref[0]
    kext[BLOCK:BLOCK + t] = kc_ref[0]
    kext[BLOCK + t:] = kn_ref[0]
    vext[0:BLOCK] = vp_ref[0]
    vext[BLOCK:BLOCK + t] = vc_ref[0]
    vext[BLOCK + t:] = vn_ref[0]

    row = lax.broadcasted_iota(jnp.int32, (BLOCK, BLOCK), 0)
    col = lax.broadcasted_iota(jnp.int32, (BLOCK, BLOCK), 1)
    no_cap = float(jnp.finfo(F32).max)
    cap_ref[1] = jnp.where(col < row, NEG_INF, no_cap)
    cap_ref[2] = jnp.where(col > row, NEG_INF, no_cap)
    cap_ref[0] = jnp.where(col < jnp.maximum(row, jnp.where(i == 0, BLOCK, 0)), NEG_INF, no_cap)
    cap_ref[3] = jnp.where(col > jnp.minimum(row, jnp.where(i == n_tiles - 1, -1, BLOCK)),
                           NEG_INF, no_cap)

    for gi, w in enumerate(POOL_WINDOWS):
        half = w // 2
        lanes = slice(gi * POOL_GROUP_DIM, (gi + 1) * POOL_GROUP_DIM)
        uext[gi, 0:POOL_HALO] = jnp.where(i > 0, up_ref[0, :, lanes], 0.0)
        uext[gi, POOL_HALO:POOL_HALO + t] = uc_ref[0, :, lanes]
        uext[gi, POOL_HALO + t:] = jnp.where(i < n_tiles - 1, un_ref[0, :, lanes], 0.0)
        wsum = uext[gi, POOL_HALO - half:POOL_HALO - half + t]
        for d in range(-half + 1, half):
            wsum = wsum + uext[gi, POOL_HALO + d:POOL_HALO + d + t]
        centre = uc_ref[0, :, lanes]

        def edge(r):
            pos = i * t + r + lax.broadcasted_iota(jnp.int32, (POOL_HALO, LANES), 0)
            cnt = jnp.minimum(pos + half, seq_len) - jnp.maximum(pos - half, 0)
            return wsum[r:r + POOL_HALO] / cnt.astype(F32) - centre[r:r + POOL_HALO]

        inner = slice(POOL_HALO, t - POOL_HALO)
        pooled = jnp.concatenate(
            [edge(0), wsum[inner] * (1.0 / w) - centre[inner], edge(t - POOL_HALO)], axis=0)
        mixed = jnp.dot(pooled.astype(BF16), wpool_ref[gi], preferred_element_type=F32)
        mixed = mixed * pscale_ref[:, lanes]
        mix_ref[:, ATTN_WIDTH + gi * POOL_GROUP_DIM:ATTN_WIDTH + (gi + 1) * POOL_GROUP_DIM] = (
            mixed.astype(BF16))

    for jb in range(n_blocks):
        r0 = jb * BLOCK
        slot = jb % 2
        cap_lo = 0 if jb == 0 else 1
        cap_hi = 3 if jb == n_blocks - 1 else 2

        for ge in range(4):
            group = ge // 2
            q2 = jnp.concatenate(
                [q_ref[0, r0:r0 + BLOCK, (2 * group + cc) * LANES:(2 * group + cc + 1) * LANES]
                 for cc in range(2)], axis=0)
            kband = kext[r0:r0 + BAND, ge * LANES:(ge + 1) * LANES]
            s_ref[slot, ge] = lax.dot_general(q2, kband, (((1,), (1,)), ((), ())),
                                              preferred_element_type=F32)

        rinv = {}
        for ge in range(4):
            group, e = ge // 2, ge % 2
            for cc in range(2):
                rows = slice(cc * BLOCK, (cc + 1) * BLOCK)
                sink = sink_ref[2 * (2 * group + cc) + e]
                s0 = jnp.minimum(s_ref[slot, ge, rows, 0:BLOCK], cap_ref[cap_lo])
                s1 = s_ref[slot, ge, rows, BLOCK:2 * BLOCK]
                s2 = jnp.minimum(s_ref[slot, ge, rows, 2 * BLOCK:], cap_ref[cap_hi])
                m = jnp.max(jnp.maximum(jnp.maximum(s0, s1), s2), axis=-1, keepdims=True)
                m = jnp.maximum(m, sink)
                p0 = jnp.exp(s0 - m)
                p1 = jnp.exp(s1 - m)
                p2 = jnp.exp(s2 - m)
                denom = jnp.sum(p0 + p1 + p2, axis=-1, keepdims=True) + jnp.exp(sink - m)
                p_ref[slot, ge, rows, 0:BLOCK] = p0.astype(BF16)
                p_ref[slot, ge, rows, BLOCK:2 * BLOCK] = p1.astype(BF16)
                p_ref[slot, ge, rows, 2 * BLOCK:] = p2.astype(BF16)
                rinv[ge, cc] = 1.0 / denom

        for group in range(2):
            acc = [None, None]
            for e in range(2):
                ge = 2 * group + e
                vband = vext[r0:r0 + BAND, ge * LANES:(ge + 1) * LANES]
                o2 = jnp.dot(p_ref[slot, ge], vband, preferred_element_type=F32)
                for cc in range(2):
                    o = o2[cc * BLOCK:(cc + 1) * BLOCK] * rinv[ge, cc]
                    acc[cc] = o if acc[cc] is None else acc[cc] + o
            for cc in range(2):
                c = 2 * group + cc
                mix_ref[r0:r0 + BLOCK, c * LANES:(c + 1) * LANES] = acc[cc].astype(BF16)

    for rows in _row_halves(t):
        m = jnp.dot(mix_ref[rows, :], wout_ref[...], preferred_element_type=F32)
        o_ref[0, rows, :] = x_ref[0, rows, :] + _rmsnorm(m, g_ref[...])


def _attn_pool(x, q, k4, v4, u, sink, w_pool_b, pool_scale, w_out_b, g_post):
    b, s, _ = x.shape
    t = TOKEN_TILE
    per_blk = t // BLOCK
    n_blk = s // BLOCK
    per_halo = t // POOL_HALO
    n_halo = s // POOL_HALO
    tok = lambda width: pl.BlockSpec((1, t, width), lambda bi, i: (bi, i, 0))
    blk_prev = pl.BlockSpec((1, BLOCK, 4 * LANES),
                            lambda bi, i: (bi, jnp.maximum(i * per_blk - 1, 0), 0))
    blk_next = pl.BlockSpec((1, BLOCK, 4 * LANES),
                            lambda bi, i: (bi, jnp.minimum((i + 1) * per_blk, n_blk - 1), 0))
    halo_prev = pl.BlockSpec((1, POOL_HALO, POOL_WIDTH),
                             lambda bi, i: (bi, jnp.maximum(i * per_halo - 1, 0), 0))
    halo_next = pl.BlockSpec((1, POOL_HALO, POOL_WIDTH),
                             lambda bi, i: (bi, jnp.minimum((i + 1) * per_halo, n_halo - 1), 0))
    return pl.pallas_call(
        functools.partial(_attn_pool_kernel, s),
        grid=(b, s // t),
        in_specs=[pl.BlockSpec(memory_space=pltpu.SMEM),
                  tok(D_MODEL), tok(ATTN_WIDTH),
                  blk_prev, tok(4 * LANES), blk_next,
                  blk_prev, tok(4 * LANES), blk_next,
                  halo_prev, tok(POOL_WIDTH), halo_next,
                  _const_spec((len(POOL_WINDOWS), POOL_GROUP_DIM, POOL_GROUP_DIM)),
                  _const_spec((1, POOL_WIDTH)),
                  _const_spec((ATTN_WIDTH + POOL_WIDTH, D_MODEL)),
                  _const_spec((1, D_MODEL))],
        out_specs=tok(D_MODEL),
        out_shape=jax.ShapeDtypeStruct((b, s, D_MODEL), F32),
        scratch_shapes=[pltpu.VMEM((t + 2 * BLOCK, 4 * LANES), BF16),
                        pltpu.VMEM((t + 2 * BLOCK, 4 * LANES), BF16),
                        pltpu.VMEM((len(POOL_WINDOWS), t + 2 * POOL_HALO, LANES), F32),
                        pltpu.VMEM((4, BLOCK, BLOCK), F32),
                        pltpu.VMEM((2, 4, 2 * BLOCK, BAND), F32),
                        pltpu.VMEM((2, 4, 2 * BLOCK, BAND), BF16),
                        pltpu.VMEM((t, ATTN_WIDTH + POOL_WIDTH), BF16)],
        compiler_params=_params(2),
        name="attn_pool_outproj",
    )(sink, x, q, k4, k4, k4, v4, v4, v4, u, u, u, w_pool_b, pool_scale, w_out_b, g_post)


def _swiglu_kernel(x_ref, gpre_ref, wg_ref, wu_ref, wd_ref, gpost_ref, o_ref, act_ref):
    for rows in _row_halves(x_ref.shape[0]):
        hb = _rmsnorm(x_ref[rows, :], gpre_ref[...]).astype(BF16)
        for c in range(D_FF // FF_CHUNK):
            cols = slice(c * FF_CHUNK, (c + 1) * FF_CHUNK)
            gate = jnp.dot(hb, wg_ref[:, cols], preferred_element_type=F32)
            up = jnp.dot(hb, wu_ref[:, cols], preferred_element_type=F32)
            act = (gate * jax.nn.sigmoid(gate)) * up
            act_ref[rows, cols] = act.astype(BF16)
        f = jnp.dot(act_ref[rows, :], wd_ref[...], preferred_element_type=F32)
        o_ref[rows, :] = x_ref[rows, :] + _rmsnorm(f, gpost_ref[...])


def _swiglu(x2d, g_pre, wg_b, wu_b, wd_b, g_post, layer):
    n, _ = x2d.shape
    t = WIDE_TILE
    tok = pl.BlockSpec((t, D_MODEL), lambda i: (i, 0))
    layer_spec = lambda rows, cols: pl.BlockSpec((None, rows, cols), lambda i: (layer, 0, 0),
                                                 pipeline_mode=pl.Buffered(1))
    return pl.pallas_call(
        _swiglu_kernel,
        grid=(n // t,),
        in_specs=[tok, _const_spec((1, D_MODEL)),
                  layer_spec(D_MODEL, D_FF), layer_spec(D_MODEL, D_FF),
                  layer_spec(D_FF, D_MODEL), _const_spec((1, D_MODEL))],
        out_specs=tok,
        out_shape=jax.ShapeDtypeStruct((n, D_MODEL), F32),
        scratch_shapes=[pltpu.VMEM((t, D_FF), BF16)],
        compiler_params=_params(1),
        name="swiglu",
    )(x2d, g_pre, wg_b, wu_b, wd_b, g_post)


def _tree_sum(terms):
    while len(terms) > 1:
        terms = [terms[n] + terms[n + 1] if n + 1 < len(terms) else terms[n]
                 for n in range(0, len(terms), 2)]
    return terms[0]


def _ff_cols(c):
    start = c * FF_CHUNK
    return pl.ds(start if isinstance(c, int) else pl.multiple_of(start, FF_CHUNK), FF_CHUNK)


def _ffn_products(c, hb_ref, wg_ref, wu_ref, gu_ref):
    gu_ref[0] = jnp.dot(hb_ref[...], wg_ref[:, _ff_cols(c)], preferred_element_type=F32)
    gu_ref[1] = jnp.dot(hb_ref[...], wu_ref[:, _ff_cols(c)], preferred_element_type=F32)


def _ffn_activate(c, gu_ref, act_ref):
    gate = gu_ref[0]
    act_ref[:, _ff_cols(c)] = ((gate * jax.nn.sigmoid(gate)) * gu_ref[1]).astype(BF16)


def _conv_ffn_kernel(tiles_per_seq, xp_ref, xc_ref, xn_ref, gpre_ref, w1_ref, b1_ref, wdw_ref,
                     bdw_ref, lng_ref, lnb_ref, w2_ref, b2_ref, gpost_ref, fpre_ref, wg_ref, wu_ref,
                     wd_ref, fpost_ref, o_ref, hcat_ref, gch, gpk, ych, x1_buf, hb_ref, gu_even,
                     gu_odd, act_ref):
    t = xc_ref.shape[1]
    j = pl.program_id(0)
    tile = jnp.minimum(j, pl.num_programs(0) - 2)
    i = lax.rem(tile, tiles_per_seq)
    slot = lax.rem(j, 2)

    @pl.when(j == 0)
    def _():
        x1_buf[1] = jnp.zeros((t, D_MODEL), F32)

    hb_ref[...] = _rmsnorm(x1_buf[1 - slot], fpre_ref[...]).astype(BF16)

    gpre = gpre_ref[...]
    hcat_ref[0:CONV_HALO] = _rmsnorm(xp_ref[0], gpre).astype(BF16)
    hcat_ref[CONV_HALO:CONV_HALO + t] = _rmsnorm(xc_ref[0], gpre).astype(BF16)
    hcat_ref[CONV_HALO + t:] = _rmsnorm(xn_ref[0], gpre).astype(BF16)
    rowid = lax.broadcasted_iota(jnp.int32, (t + 2 * CONV_HALO, 1), 0)
    first_row = jnp.where(i > 0, 0, CONV_HALO)
    end_row = jnp.where(i < tiles_per_seq - 1, t + 2 * CONV_HALO, t + CONV_HALO)
    inside = (rowid >= first_row) & (rowid < end_row)
    for cp in range(N_CHUNKS // 2):
        vcols = slice(cp * 2 * LANES, (cp + 1) * 2 * LANES)
        gcols = slice(D_MODEL + cp * 2 * LANES, D_MODEL + (cp + 1) * 2 * LANES)
        zv = jnp.dot(hcat_ref[...], w1_ref[:, vcols], preferred_element_type=F32) + b1_ref[:, vcols]
        zg = jnp.dot(hcat_ref[...], w1_ref[:, gcols], preferred_element_type=F32) + b1_ref[:, gcols]
        glu = jnp.where(inside, zv * jax.nn.sigmoid(zg), 0.0)
        gch[2 * cp] = glu[:, :LANES]
        gch[2 * cp + 1] = glu[:, LANES:]

    n_pairs = gpk.shape[2]

    gu = (gu_even, gu_odd)
    _ffn_products(0, hb_ref, wg_ref, wu_ref, gu[0])

    def paired_unit(c, parity_c):
        _ffn_activate(c, gu[parity_c], act_ref)
        _ffn_products(c + 1, hb_ref, wg_ref, wu_ref, gu[1 - parity_c])
        for parity in range(2):
            rows = gch[c, parity:parity + 2 * n_pairs, :].astype(BF16)
            gpk[parity, c] = pltpu.bitcast(rows, jnp.uint32)
        wk = wdw_ref[c].astype(BF16)
        bias = bdw_ref[c]
        for rb in range(t // CONV_ROWS):
            base = rb * CONV_ROWS + CONV_HALO - CONV_WIDTH // 2
            acc = jnp.broadcast_to(bias, (CONV_ROWS, LANES))
            for k0 in range(0, CONV_WIDTH, CONV_TAP_GROUP):
                prods = []
                for k in range(k0, min(k0 + CONV_TAP_GROUP, CONV_WIDTH)):
                    start = base + k
                    window = gpk[start % 2, c, start // 2:start // 2 + CONV_ROWS // 2, :]
                    prods.append(pltpu.bitcast(window, BF16) * wk[k:k + 1, :])
                acc = acc + _tree_sum(prods).astype(F32)
            ych[c, rb * CONV_ROWS:(rb + 1) * CONV_ROWS, :] = acc

    def unit_pair(n, carry):
        paired_unit(2 * n, 0)
        paired_unit(2 * n + 1, 1)
        return carry

    lax.fori_loop(0, N_CHUNKS // 2, unit_pair, 0)
    for c in range(N_CHUNKS, D_FF // FF_CHUNK):
        if c > N_CHUNKS:
            _ffn_products(c, hb_ref, wg_ref, wu_ref, gu[c % 2])
        _ffn_activate(c, gu[c % 2], act_ref)

    y = jnp.concatenate([ych[c] for c in range(N_CHUNKS)], axis=1)
    mu = jnp.mean(y, axis=-1, keepdims=True)
    var = jnp.mean(jnp.square(y - mu), axis=-1, keepdims=True)
    yn = (y - mu) * lax.rsqrt(var + EPS) * lng_ref[...] + lnb_ref[...]
    act = (yn * jax.nn.sigmoid(yn)).astype(BF16)
    m = jnp.dot(act, w2_ref[...], preferred_element_type=F32) + b2_ref[...]
    x1_buf[slot] = xc_ref[0] + _rmsnorm(m, gpost_ref[...])

    f = jnp.dot(act_ref[...], wd_ref[...], preferred_element_type=F32)
    o_ref[0] = x1_buf[1 - slot] + _rmsnorm(f, fpost_ref[...])


def _conv_ffn(x, g_pre, w1_b, b1, wdw, bdw, ln_g, ln_b, w2_b, b2, g_post, f_pre, wg_b, wu_b, wd_b,
              f_post, layer):
    b, s, _ = x.shape
    t = TOKEN_TILE
    per_seq = s // t
    n_tiles = b * per_seq
    per_halo = t // CONV_HALO
    n_halo = s // CONV_HALO

    def conv_tile(j):
        tile = jnp.minimum(j, n_tiles - 1)
        return tile // per_seq, tile % per_seq

    def cur(j):
        bi, i = conv_tile(j)
        return bi, i, 0

    def prev(j):
        bi, i = conv_tile(j)
        return bi, jnp.maximum(i * per_halo - 1, 0), 0

    def nxt(j):
        bi, i = conv_tile(j)
        return bi, jnp.minimum((i + 1) * per_halo, n_halo - 1), 0

    def out(j):
        tile = jnp.maximum(j - 1, 0)
        return tile // per_seq, tile % per_seq, 0

    halo = lambda index_map: pl.BlockSpec((1, CONV_HALO, D_MODEL), index_map)
    layer_spec = lambda rows, cols: pl.BlockSpec((None, rows, cols), lambda j: (layer, 0, 0),
                                                 pipeline_mode=pl.Buffered(1))
    return pl.pallas_call(
        functools.partial(_conv_ffn_kernel, per_seq),
        grid=(n_tiles + 1,),
        in_specs=[halo(prev), pl.BlockSpec((1, t, D_MODEL), cur), halo(nxt),
                  _const_spec((1, D_MODEL)),
                  _const_spec((D_MODEL, 2 * D_MODEL)), _const_spec((1, 2 * D_MODEL)),
                  _const_spec((N_CHUNKS, CONV_WIDTH, LANES)), _const_spec((N_CHUNKS, 1, LANES)),
                  _const_spec((1, D_MODEL)), _const_spec((1, D_MODEL)),
                  _const_spec((D_MODEL, D_MODEL)), _const_spec((1, D_MODEL)),
                  _const_spec((1, D_MODEL)),
                  _const_spec((1, D_MODEL)),
                  layer_spec(D_MODEL, D_FF), layer_spec(D_MODEL, D_FF),
                  layer_spec(D_FF, D_MODEL), _const_spec((1, D_MODEL))],
        out_specs=pl.BlockSpec((1, t, D_MODEL), out),
        out_shape=jax.ShapeDtypeStruct((b, s, D_MODEL), F32),
        scratch_shapes=[pltpu.VMEM((t + 2 * CONV_HALO, D_MODEL), BF16),
                        pltpu.VMEM((N_CHUNKS, t + 2 * CONV_HALO, LANES), F32),
                        pltpu.VMEM((2, N_CHUNKS, (t + 2 * CONV_HALO) // 2 - 1, LANES), jnp.uint32),
                        pltpu.VMEM((N_CHUNKS, t, LANES), F32),
                        pltpu.VMEM((2, t, D_MODEL), F32),
                        pltpu.VMEM((t, D_MODEL), BF16),
                        pltpu.VMEM((2, t, FF_CHUNK), F32),
                        pltpu.VMEM((2, t, FF_CHUNK), F32),
                        pltpu.VMEM((t, D_FF), BF16)],
        compiler_params=pltpu.CompilerParams(dimension_semantics=("arbitrary",),
                                             vmem_limit_bytes=VMEM_LIMIT_BYTES),
        name="conv_ffn",
    )(x, x, x, g_pre, w1_b, b1, wdw, bdw, ln_g, ln_b, w2_b, b2, g_post, f_pre, wg_b, wu_b, wd_b,
      f_post)


def _rope_tables(seq_len):
    half = HEAD_DIM // 2
    inv_freq = ROPE_THETA ** (-jnp.arange(0, half, dtype=F32) * 2.0 / HEAD_DIM)
    ang = jnp.arange(seq_len).astype(F32)[:, None] * inv_freq[None, :]
    cos = jnp.cos(ang)
    sin = jnp.sin(ang)
    return jnp.tile(cos, (1, 4)), jnp.concatenate([-sin, sin, -sin, sin], axis=1)


def _chunk_major(w):
    rows = w.shape[0]
    return w.reshape(rows, N_CHUNKS, LANES).transpose(1, 0, 2)


def kernel(x_prompt, x_sample, mix_pre_g, mix_post_g, ffn_pre_g, ffn_post_g, w_in, attn_sink,
           w_pool, pool_scale, w_out, conv_w_pw1, conv_b_pw1, conv_w_dw, conv_b_dw, conv_ln_g,
           conv_ln_b, conv_w_pw2, conv_b_pw2, ffn_w_gate, ffn_w_up, ffn_w_down):
    depth = mix_pre_g.shape[0]
    row = lambda v: v.reshape(1, -1)
    col_scale = jnp.concatenate([jnp.full((ATTN_WIDTH,), HEAD_DIM ** -0.5, F32),
                                 jnp.ones((IN_PROJ_WIDTH - ATTN_WIDTH,), F32)])
    w_in_b = (w_in * col_scale).astype(BF16)
    w_pool_b = w_pool.astype(BF16)
    w_out_b = w_out.astype(BF16)
    w1_b = conv_w_pw1.astype(BF16)
    w2_b = conv_w_pw2.astype(BF16)
    wg_b = ffn_w_gate.astype(BF16)
    wu_b = ffn_w_up.astype(BF16)
    wd_b = ffn_w_down.astype(BF16)

    def trunk(x):
        b, s, _ = x.shape
        cos, sin = _rope_tables(s)
        for layer in range(depth):
            if layer % 2 == 0:
                e = layer // 2
                q, k4, v4, u = _inproj(x, row(mix_pre_g[layer]), w_in_b[e], cos, sin)
                x = _attn_pool(x, q, k4, v4, u, attn_sink[e], w_pool_b[e], row(pool_scale[e]),
                               w_out_b[e], row(mix_post_g[layer]))
                x = _swiglu(x.reshape(b * s, D_MODEL), row(ffn_pre_g[layer]), wg_b, wu_b, wd_b,
                            row(ffn_post_g[layer]), layer).reshape(b, s, D_MODEL)
            else:
                o = layer // 2
                x = _conv_ffn(x, row(mix_pre_g[layer]), w1_b[o], row(conv_b_pw1[o]),
                              _chunk_major(conv_w_dw[o]),
                              conv_b_dw[o].reshape(N_CHUNKS, 1, LANES),
                              row(conv_ln_g[o]), row(conv_ln_b[o]), w2_b[o],
                              row(conv_b_pw2[o]), row(mix_post_g[layer]),
                              row(ffn_pre_g[layer]), wg_b, wu_b, wd_b, row(ffn_post_g[layer]),
                              layer)
        return x

    return (trunk(x_prompt), trunk(x_sample))
```

```python
import functools

import jax
import jax.numpy as jnp
from jax import lax
from jax.experimental import pallas as pl
from jax.experimental.pallas import tpu as pltpu

D_MODEL = 1024
HEAD_DIM = 64
N_Q_HEADS = 8
N_KV_HEADS = 2
ATTN_WIDTH = N_Q_HEADS * HEAD_DIM
KV_WIDTH = N_KV_HEADS * HEAD_DIM
WINDOW = 128
BLOCK = 128
BAND = 3 * BLOCK
ROPE_THETA = 10000.0
POOL_WINDOWS = (2, 4, 8, 16)
POOL_GROUP_DIM = 128
POOL_WIDTH = 512
POOL_HALO = 8
IN_PROJ_WIDTH = ATTN_WIDTH + 2 * KV_WIDTH + POOL_WIDTH
CONV_WIDTH = 31
CONV_HALO = 16
D_FF = 2816
EPS = 1e-6
NEG_INF = -1e30

LANES = 128
N_CHUNKS = D_MODEL // LANES
TOKEN_TILE = 1024
FF_CHUNK = 256
CONV_ROWS = 64
CONV_TAP_GROUP = 8
VMEM_LIMIT_BYTES = 56 * 1024 * 1024

F32 = jnp.float32
BF16 = jnp.bfloat16


def _rmsnorm(x, g):
    ms = jnp.mean(x * x, axis=-1, keepdims=True)
    return (x * lax.rsqrt(ms + EPS)) * g


def _params(n_axes):
    return pltpu.CompilerParams(
        dimension_semantics=("parallel",) * n_axes, vmem_limit_bytes=VMEM_LIMIT_BYTES)


def _row_halves(t):
    return (slice(0, t // 2), slice(t // 2, t))


def _const_spec(shape):
    zeros = (0,) * len(shape)
    return pl.BlockSpec(shape, lambda *_: zeros, pipeline_mode=pl.Buffered(1))


def _inproj_kernel(x_ref, g_ref, w_ref, cos_ref, sin_ref, q_ref, k4_ref, v4_ref, u_ref):
    t = x_ref.shape[1]
    h = _rmsnorm(x_ref[0], g_ref[...])
    z = jnp.dot(h.astype(BF16), w_ref[...], preferred_element_type=F32)
    cos = cos_ref[...]
    sin = sin_ref[...]
    lane = lax.broadcasted_iota(jnp.int32, (t, LANES), 1)
    first_half = (lane & (HEAD_DIM - 1)) < HEAD_DIM // 2
    low = lane < HEAD_DIM

    def rope(zc):
        swapped = jnp.where(first_half, pltpu.roll(zc, LANES - HEAD_DIM // 2, 1),
                            pltpu.roll(zc, HEAD_DIM // 2, 1))
        return zc * cos + swapped * sin

    def pad_heads(a):
        ar = pltpu.roll(a, HEAD_DIM, 1)
        zero = jnp.zeros_like(a)
        return (jnp.where(low, a, zero), jnp.where(low, zero, ar),
                jnp.where(low, ar, zero), jnp.where(low, zero, a))

    for c in range(ATTN_WIDTH // LANES):
        q_ref[0, :, c * LANES:(c + 1) * LANES] = rope(z[:, c * LANES:(c + 1) * LANES]).astype(BF16)
    k = rope(z[:, ATTN_WIDTH:ATTN_WIDTH + KV_WIDTH])
    v = z[:, ATTN_WIDTH + KV_WIDTH:ATTN_WIDTH + 2 * KV_WIDTH]
    for j, (kp, vp) in enumerate(zip(pad_heads(k), pad_heads(v))):
        k4_ref[0, :, j * LANES:(j + 1) * LANES] = kp.astype(BF16)
        v4_ref[0, :, j * LANES:(j + 1) * LANES] = vp.astype(BF16)
    u_ref[0] = z[:, ATTN_WIDTH + 2 * KV_WIDTH:]


def _inproj(x, g, w_in_b, cos, sin):
    b, s, _ = x.shape
    t = TOKEN_TILE
    tok = lambda width: pl.BlockSpec((1, t, width), lambda bi, i: (bi, i, 0))
    table = pl.BlockSpec((t, LANES), lambda bi, i: (i, 0))
    return pl.pallas_call(
        _inproj_kernel,
        grid=(b, s // t),
        in_specs=[tok(D_MODEL), _const_spec((1, D_MODEL)), _const_spec((D_MODEL, IN_PROJ_WIDTH)),
                  table, table],
        out_specs=[tok(ATTN_WIDTH), tok(4 * LANES), tok(4 * LANES), tok(POOL_WIDTH)],
        out_shape=[jax.ShapeDtypeStruct((b, s, ATTN_WIDTH), BF16),
                   jax.ShapeDtypeStruct((b, s, 4 * LANES), BF16),
                   jax.ShapeDtypeStruct((b, s, 4 * LANES), BF16),
                   jax.ShapeDtypeStruct((b, s, POOL_WIDTH), F32)],
        compiler_params=_params(2),
        name="inproj_rope",
    )(x, g, w_in_b, cos, sin)


def _attn_pool_kernel(seq_len, sink_ref, x_ref, q_ref, kp_ref, kc_ref, kn_ref, vp_ref, vc_ref,
                      vn_ref, up_ref, uc_ref, un_ref, wpool_ref, pscale_ref, wout_ref, g_ref,
                      o_ref, kext, vext, uext, cap_ref, s_ref, p_ref, mix_ref):
    t = x_ref.shape[1]
    i = pl.program_id(1)
    n_tiles = pl.num_programs(1)
    n_blocks = t // BLOCK

    kext[0:BLOCK] = kp_ref[0]
    kext[BLOCK:BLOCK + t] = kc_ref[0]
    kext[BLOCK + t:] = kn_ref[0]
    vext[0:BLOCK] = vp_ref[0]
    vext[BLOCK:BLOCK + t] = vc_ref[0]
    vext[BLOCK + t:] = vn_ref[0]

    row = lax.broadcasted_iota(jnp.int32, (BLOCK, BLOCK), 0)
    col = lax.broadcasted_iota(jnp.int32, (BLOCK, BLOCK), 1)
    no_cap = float(jnp.finfo(F32).max)
    cap_ref[1] = jnp.where(col < row, NEG_INF, no_cap)
    cap_ref[2] = jnp.where(col > row, NEG_INF, no_cap)
    cap_ref[0] = jnp.where(col < jnp.maximum(row, jnp.where(i == 0, BLOCK, 0)), NEG_INF, no_cap)
    cap_ref[3] = jnp.where(col > jnp.minimum(row, jnp.where(i == n_tiles - 1, -1, BLOCK)),
                           NEG_INF, no_cap)

    for gi, w in enumerate(POOL_WINDOWS):
        half = w // 2
        lanes = slice(gi * POOL_GROUP_DIM, (gi + 1) * POOL_GROUP_DIM)
        uext[gi, 0:POOL_HALO] = jnp.where(i > 0, up_ref[0, :, lanes], 0.0)
        uext[gi, POOL_HALO:POOL_HALO + t] = uc_ref[0, :, lanes]
        uext[gi, POOL_HALO + t:] = jnp.where(i < n_tiles - 1, un_ref[0, :, lanes], 0.0)
        wsum = uext[gi, POOL_HALO - half:POOL_HALO - half + t]
        for d in range(-half + 1, half):
            wsum = wsum + uext[gi, POOL_HALO + d:POOL_HALO + d + t]
        centre = uc_ref[0, :, lanes]

        def edge(r):
            pos = i * t + r + lax.broadcasted_iota(jnp.int32, (POOL_HALO, LANES), 0)
            cnt = jnp.minimum(pos + half, seq_len) - jnp.maximum(pos - half, 0)
            return wsum[r:r + POOL_HALO] / cnt.astype(F32) - centre[r:r + POOL_HALO]

        inner = slice(POOL_HALO, t - POOL_HALO)
        pooled = jnp.concatenate(
            [edge(0), wsum[inner] * (1.0 / w) - centre[inner], edge(t - POOL_HALO)], axis=0)
        mixed = jnp.dot(pooled.astype(BF16), wpool_ref[gi], preferred_element_type=F32)
        mixed = mixed * pscale_ref[:, lanes]
        mix_ref[:, ATTN_WIDTH + gi * POOL_GROUP_DIM:ATTN_WIDTH + (gi + 1) * POOL_GROUP_DIM] = (
            mixed.astype(BF16))

    for jb in range(n_blocks):
        r0 = jb * BLOCK
        slot = jb % 2
        cap_lo = 0 if jb == 0 else 1
        cap_hi = 3 if jb == n_blocks - 1 else 2

        for ge in range(4):
            group = ge // 2
            q2 = jnp.concatenate(
                [q_ref[0, r0:r0 + BLOCK, (2 * group + cc) * LANES:(2 * group + cc + 1) * LANES]
                 for cc in range(2)], axis=0)
            kband = kext[r0:r0 + BAND, ge * LANES:(ge + 1) * LANES]
            s_ref[slot, ge] = lax.dot_general(q2, kband, (((1,), (1,)), ((), ())),
                                              preferred_element_type=F32)

        rinv = {}
        for ge in range(4):
            group, e = ge // 2, ge % 2
            for cc in range(2):
                rows = slice(cc * BLOCK, (cc + 1) * BLOCK)
                sink = sink_ref[2 * (2 * group + cc) + e]
                s0 = jnp.minimum(s_ref[slot, ge, rows, 0:BLOCK], cap_ref[cap_lo])
                s1 = s_ref[slot, ge, rows, BLOCK:2 * BLOCK]
                s2 = jnp.minimum(s_ref[slot, ge, rows, 2 * BLOCK:], cap_ref[cap_hi])
                m = jnp.max(jnp.maximum(jnp.maximum(s0, s1), s2), axis=-1, keepdims=True)
                m = jnp.maximum(m, sink)
                p0 = jnp.exp(s0 - m)
                p1 = jnp.exp(s1 - m)
                p2 = jnp.exp(s2 - m)
                denom = jnp.sum(p0 + p1 + p2, axis=-1, keepdims=True) + jnp.exp(sink - m)
                p_ref[slot, ge, rows, 0:BLOCK] = p0.astype(BF16)
                p_ref[slot, ge, rows, BLOCK:2 * BLOCK] = p1.astype(BF16)
                p_ref[slot, ge, rows, 2 * BLOCK:] = p2.astype(BF16)
                rinv[ge, cc] = 1.0 / denom

        for group in range(2):
            acc = [None, None]
            for e in range(2):
                ge = 2 * group + e
                vband = vext[r0:r0 + BAND, ge * LANES:(ge + 1) * LANES]
                o2 = jnp.dot(p_ref[slot, ge], vband, preferred_element_type=F32)
                for cc in range(2):
                    o = o2[cc * BLOCK:(cc + 1) * BLOCK] * rinv[ge, cc]
                    acc[cc] = o if acc[cc] is None else acc[cc] + o
            for cc in range(2):
                c = 2 * group + cc
                mix_ref[r0:r0 + BLOCK, c * LANES:(c + 1) * LANES] = acc[cc].astype(BF16)

    for rows in _row_halves(t):
        m = jnp.dot(mix_ref[rows, :], wout_ref[...], preferred_element_type=F32)
        o_ref[0, rows, :] = x_ref[0, rows, :] + _rmsnorm(m, g_ref[...])


def _attn_pool(x, q, k4, v4, u, sink, w_pool_b, pool_scale, w_out_b, g_post):
    b, s, _ = x.shape
    t = TOKEN_TILE
    per_blk = t // BLOCK
    n_blk = s // BLOCK
    per_halo = t // POOL_HALO
    n_halo = s // POOL_HALO
    tok = lambda width: pl.BlockSpec((1, t, width), lambda bi, i: (bi, i, 0))
    blk_prev = pl.BlockSpec((1, BLOCK, 4 * LANES),
                            lambda bi, i: (bi, jnp.maximum(i * per_blk - 1, 0), 0))
    blk_next = pl.BlockSpec((1, BLOCK, 4 * LANES),
                            lambda bi, i: (bi, jnp.minimum((i + 1) * per_blk, n_blk - 1), 0))
    halo_prev = pl.BlockSpec((1, POOL_HALO, POOL_WIDTH),
                             lambda bi, i: (bi, jnp.maximum(i * per_halo - 1, 0), 0))
    halo_next = pl.BlockSpec((1, POOL_HALO, POOL_WIDTH),
                             lambda bi, i: (bi, jnp.minimum((i + 1) * per_halo, n_halo - 1), 0))
    return pl.pallas_call(
        functools.partial(_attn_pool_kernel, s),
        grid=(b, s // t),
        in_specs=[pl.BlockSpec(memory_space=pltpu.SMEM),
                  tok(D_MODEL), tok(ATTN_WIDTH),
                  blk_prev, tok(4 * LANES), blk_next,
                  blk_prev, tok(4 * LANES), blk_next,
                  halo_prev, tok(POOL_WIDTH), halo_next,
                  _const_spec((len(POOL_WINDOWS), POOL_GROUP_DIM, POOL_GROUP_DIM)),
                  _const_spec((1, POOL_WIDTH)),
                  _const_spec((ATTN_WIDTH + POOL_WIDTH, D_MODEL)),
                  _const_spec((1, D_MODEL))],
        out_specs=tok(D_MODEL),
        out_shape=jax.ShapeDtypeStruct((b, s, D_MODEL), F32),
        scratch_shapes=[pltpu.VMEM((t + 2 * BLOCK, 4 * LANES), BF16),
                        pltpu.VMEM((t + 2 * BLOCK, 4 * LANES), BF16),
                        pltpu.VMEM((len(POOL_WINDOWS), t + 2 * POOL_HALO, LANES), F32),
                        pltpu.VMEM((4, BLOCK, BLOCK), F32),
                        pltpu.VMEM((2, 4, 2 * BLOCK, BAND), F32),
                        pltpu.VMEM((2, 4, 2 * BLOCK, BAND), BF16),
                        pltpu.VMEM((t, ATTN_WIDTH + POOL_WIDTH), BF16)],
        compiler_params=_params(2),
        name="attn_pool_outproj",
    )(sink, x, q, k4, k4, k4, v4, v4, v4, u, u, u, w_pool_b, pool_scale, w_out_b, g_post)


def _swiglu_kernel(x_ref, gpre_ref, wg_ref, wu_ref, wd_ref, gpost_ref, o_ref, act_ref):
    x = x_ref[...]
    hb = _rmsnorm(x, gpre_ref[...]).astype(BF16)
    for c in range(D_FF // FF_CHUNK):
        cols = slice(c * FF_CHUNK, (c + 1) * FF_CHUNK)
        gate = jnp.dot(hb, wg_ref[:, cols], preferred_element_type=F32)
        up = jnp.dot(hb, wu_ref[:, cols], preferred_element_type=F32)
        act = (gate * jax.nn.sigmoid(gate)) * up
        act_ref[:, cols] = act.astype(BF16)
    f = jnp.dot(act_ref[...], wd_ref[...], preferred_element_type=F32)
    o_ref[...] = x + _rmsnorm(f, gpost_ref[...])


def _swiglu(x2d, g_pre, wg_b, wu_b, wd_b, g_post, layer):
    n, _ = x2d.shape
    t = TOKEN_TILE
    tok = pl.BlockSpec((t, D_MODEL), lambda i: (i, 0))
    layer_spec = lambda rows, cols: pl.BlockSpec((None, rows, cols), lambda i: (layer, 0, 0),
                                                 pipeline_mode=pl.Buffered(1))
    return pl.pallas_call(
        _swiglu_kernel,
        grid=(n // t,),
        in_specs=[tok, _const_spec((1, D_MODEL)),
                  layer_spec(D_MODEL, D_FF), layer_spec(D_MODEL, D_FF),
                  layer_spec(D_FF, D_MODEL), _const_spec((1, D_MODEL))],
        out_specs=tok,
        out_shape=jax.ShapeDtypeStruct((n, D_MODEL), F32),
        scratch_shapes=[pltpu.VMEM((t, D_FF), BF16)],
        compiler_params=_params(1),
        name="swiglu",
    )(x2d, g_pre, wg_b, wu_b, wd_b, g_post)


def _tree_sum(terms):
    while len(terms) > 1:
        terms = [terms[n] + terms[n + 1] if n + 1 < len(terms) else terms[n]
                 for n in range(0, len(terms), 2)]
    return terms[0]


def _conv_kernel(xp_ref, xc_ref, xn_ref, gpre_ref, w1_ref, b1_ref, wdw_ref, bdw_ref, lng_ref,
                 lnb_ref, w2_ref, b2_ref, gpost_ref, o_ref, gch, gpk, ych):
    t = xc_ref.shape[1]
    i = pl.program_id(1)
    n_tiles = pl.num_programs(1)
    gpre = gpre_ref[...]
    x = xc_ref[0]
    hcat = jnp.concatenate([_rmsnorm(xp_ref[0], gpre).astype(BF16),
                            _rmsnorm(x, gpre).astype(BF16),
                            _rmsnorm(xn_ref[0], gpre).astype(BF16)], axis=0)
    z = jnp.dot(hcat, w1_ref[...], preferred_element_type=F32) + b1_ref[...]
    glu = z[:, :D_MODEL] * jax.nn.sigmoid(z[:, D_MODEL:])
    rowid = lax.broadcasted_iota(jnp.int32, (t + 2 * CONV_HALO, 1), 0)
    first_row = jnp.where(i > 0, 0, CONV_HALO)
    end_row = jnp.where(i < n_tiles - 1, t + 2 * CONV_HALO, t + CONV_HALO)
    glu = jnp.where((rowid >= first_row) & (rowid < end_row), glu, 0.0)
    for c in range(N_CHUNKS):
        gch[c] = glu[:, c * LANES:(c + 1) * LANES]

    n_pairs = gpk.shape[2]

    def chunk_body(c, carry):
        for parity in range(2):
            rows = gch[c, parity:parity + 2 * n_pairs, :].astype(BF16)
            gpk[parity, c] = pltpu.bitcast(rows, jnp.uint32)
        wk = wdw_ref[c].astype(BF16)
        bias = bdw_ref[c]
        for rb in range(t // CONV_ROWS):
            base = rb * CONV_ROWS + CONV_HALO - CONV_WIDTH // 2
            acc = jnp.broadcast_to(bias, (CONV_ROWS, LANES))
            for k0 in range(0, CONV_WIDTH, CONV_TAP_GROUP):
                prods = []
                for k in range(k0, min(k0 + CONV_TAP_GROUP, CONV_WIDTH)):
                    start = base + k
                    window = gpk[start % 2, c, start // 2:start // 2 + CONV_ROWS // 2, :]
                    prods.append(pltpu.bitcast(window, BF16) * wk[k:k + 1, :])
                acc = acc + _tree_sum(prods).astype(F32)
            ych[c, rb * CONV_ROWS:(rb + 1) * CONV_ROWS, :] = acc
        return carry

    lax.fori_loop(0, N_CHUNKS, chunk_body, 0)

    y = jnp.concatenate([ych[c] for c in range(N_CHUNKS)], axis=1)
    mu = jnp.mean(y, axis=-1, keepdims=True)
    var = jnp.mean(jnp.square(y - mu), axis=-1, keepdims=True)
    yn = (y - mu) * lax.rsqrt(var + EPS) * lng_ref[...] + lnb_ref[...]
    act = (yn * jax.nn.sigmoid(yn)).astype(BF16)
    m = jnp.dot(act, w2_ref[...], preferred_element_type=F32) + b2_ref[...]
    o_ref[0] = x + _rmsnorm(m, gpost_ref[...])


def _conv_module(x, g_pre, w1_b, b1, wdw, bdw, ln_g, ln_b, w2_b, b2, g_post):
    b, s, _ = x.shape
    t = TOKEN_TILE
    per_halo = t // CONV_HALO
    n_halo = s // CONV_HALO
    tok = pl.BlockSpec((1, t, D_MODEL), lambda bi, i: (bi, i, 0))
    halo_prev = pl.BlockSpec((1, CONV_HALO, D_MODEL),
                             lambda bi, i: (bi, jnp.maximum(i * per_halo - 1, 0), 0))
    halo_next = pl.BlockSpec((1, CONV_HALO, D_MODEL),
                             lambda bi, i: (bi, jnp.minimum((i + 1) * per_halo, n_halo - 1), 0))
    return pl.pallas_call(
        _conv_kernel,
        grid=(b, s // t),
        in_specs=[halo_prev, tok, halo_next,
                  _const_spec((1, D_MODEL)),
                  _const_spec((D_MODEL, 2 * D_MODEL)), _const_spec((1, 2 * D_MODEL)),
                  _const_spec((N_CHUNKS, CONV_WIDTH, LANES)), _const_spec((N_CHUNKS, 1, LANES)),
                  _const_spec((1, D_MODEL)), _const_spec((1, D_MODEL)),
                  _const_spec((D_MODEL, D_MODEL)), _const_spec((1, D_MODEL)),
                  _const_spec((1, D_MODEL))],
        out_specs=tok,
        out_shape=jax.ShapeDtypeStruct((b, s, D_MODEL), F32),
        scratch_shapes=[pltpu.VMEM((N_CHUNKS, t + 2 * CONV_HALO, LANES), F32),
                        pltpu.VMEM((2, N_CHUNKS, (t + 2 * CONV_HALO) // 2 - 1, LANES), jnp.uint32),
                        pltpu.VMEM((N_CHUNKS, t, LANES), F32)],
        compiler_params=_params(2),
        name="conformer_conv",
    )(x, x, x, g_pre, w1_b, b1, wdw, bdw, ln_g, ln_b, w2_b, b2, g_post)


def _rope_tables(seq_len):
    half = HEAD_DIM // 2
    inv_freq = ROPE_THETA ** (-jnp.arange(0, half, dtype=F32) * 2.0 / HEAD_DIM)
    ang = jnp.arange(seq_len).astype(F32)[:, None] * inv_freq[None, :]
    cos = jnp.cos(ang)
    sin = jnp.sin(ang)
    return jnp.tile(cos, (1, 4)), jnp.concatenate([-sin, sin, -sin, sin], axis=1)


def _chunk_major(w):
    rows = w.shape[0]
    return w.reshape(rows, N_CHUNKS, LANES).transpose(1, 0, 2)


def kernel(x_prompt, x_sample, mix_pre_g, mix_post_g, ffn_pre_g, ffn_post_g, w_in, attn_sink,
           w_pool, pool_scale, w_out, conv_w_pw1, conv_b_pw1, conv_w_dw, conv_b_dw, conv_ln_g,
           conv_ln_b, conv_w_pw2, conv_b_pw2, ffn_w_gate, ffn_w_up, ffn_w_down):
    depth = mix_pre_g.shape[0]
    row = lambda v: v.reshape(1, -1)
    col_scale = jnp.concatenate([jnp.full((ATTN_WIDTH,), HEAD_DIM ** -0.5, F32),
                                 jnp.ones((IN_PROJ_WIDTH - ATTN_WIDTH,), F32)])
    w_in_b = (w_in * col_scale).astype(BF16)
    w_pool_b = w_pool.astype(BF16)
    w_out_b = w_out.astype(BF16)
    w1_b = conv_w_pw1.astype(BF16)
    w2_b = conv_w_pw2.astype(BF16)
    wg_b = ffn_w_gate.astype(BF16)
    wu_b = ffn_w_up.astype(BF16)
    wd_b = ffn_w_down.astype(BF16)

    def trunk(x):
        b, s, _ = x.shape
        cos, sin = _rope_tables(s)
        for layer in range(depth):
            if layer % 2 == 0:
                e = layer // 2
                q, k4, v4, u = _inproj(x, row(mix_pre_g[layer]), w_in_b[e], cos, sin)
                x = _attn_pool(x, q, k4, v4, u, attn_sink[e], w_pool_b[e], row(pool_scale[e]),
                               w_out_b[e], row(mix_post_g[layer]))
            else:
                o = layer // 2
                x = _conv_module(x, row(mix_pre_g[layer]), w1_b[o], row(conv_b_pw1[o]),
                                 _chunk_major(conv_w_dw[o]),
                                 conv_b_dw[o].reshape(N_CHUNKS, 1, LANES),
                                 row(conv_ln_g[o]), row(conv_ln_b[o]), w2_b[o],
                                 row(conv_b_pw2[o]), row(mix_post_g[layer]))
            x = _swiglu(x.reshape(b * s, D_MODEL), row(ffn_pre_g[layer]), wg_b, wu_b, wd_b,
                        row(ffn_post_g[layer]), layer).reshape(b, s, D_MODEL)
        return x

    return (trunk(x_prompt), trunk(x_sample))
```

```python
import functools

import jax
import jax.numpy as jnp
from jax import lax
from jax.experimental import pallas as pl
from jax.experimental.pallas import tpu as pltpu

D_MODEL = 1024
HEAD_DIM = 64
N_Q_HEADS = 8
N_KV_HEADS = 2
ATTN_WIDTH = N_Q_HEADS * HEAD_DIM
KV_WIDTH = N_KV_HEADS * HEAD_DIM
WINDOW = 128
BLOCK = 128
BAND = 3 * BLOCK
ROPE_THETA = 10000.0
POOL_WINDOWS = (2, 4, 8, 16)
POOL_GROUP_DIM = 128
POOL_WIDTH = 512
POOL_HALO = 8
IN_PROJ_WIDTH = ATTN_WIDTH + 2 * KV_WIDTH + POOL_WIDTH
CONV_WIDTH = 31
CONV_HALO = 16
D_FF = 2816
EPS = 1e-6
NEG_INF = -1e30

LANES = 128
N_CHUNKS = D_MODEL // LANES
TOKEN_TILE = 1024
FF_CHUNK = 256
CONV_ROWS = 64
CONV_TAP_GROUP = 8
VMEM_LIMIT_BYTES = 56 * 1024 * 1024

F32 = jnp.float32
BF16 = jnp.bfloat16


def _rmsnorm(x, g):
    ms = jnp.mean(x * x, axis=-1, keepdims=True)
    return (x * lax.rsqrt(ms + EPS)) * g


def _params(n_axes):
    return pltpu.CompilerParams(
        dimension_semantics=("parallel",) * n_axes, vmem_limit_bytes=VMEM_LIMIT_BYTES)


def _row_halves(t):
    return (slice(0, t // 2), slice(t // 2, t))


def _const_spec(shape):
    zeros = (0,) * len(shape)
    return pl.BlockSpec(shape, lambda *_: zeros, pipeline_mode=pl.Buffered(1))


def _inproj_kernel(x_ref, g_ref, w_ref, cos_ref, sin_ref, q_ref, k2_ref, v2_ref, u_ref):
    t = x_ref.shape[1]
    h = _rmsnorm(x_ref[0], g_ref[...])
    z = jnp.dot(h.astype(BF16), w_ref[...], preferred_element_type=F32)
    cos = cos_ref[...]
    sin = sin_ref[...]
    lane = lax.broadcasted_iota(jnp.int32, (t, LANES), 1)
    first_half = (lane & (HEAD_DIM - 1)) < HEAD_DIM // 2

    def rope(zc):
        swapped = jnp.where(first_half, pltpu.roll(zc, LANES - HEAD_DIM // 2, 1),
                            pltpu.roll(zc, HEAD_DIM // 2, 1))
        return zc * cos + swapped * sin

    for c in range(ATTN_WIDTH // LANES):
        q_ref[0, :, c * LANES:(c + 1) * LANES] = rope(z[:, c * LANES:(c + 1) * LANES]).astype(BF16)
    k = rope(z[:, ATTN_WIDTH:ATTN_WIDTH + KV_WIDTH])
    v = z[:, ATTN_WIDTH + KV_WIDTH:ATTN_WIDTH + 2 * KV_WIDTH]
    for a, a_ref in ((k, k2_ref), (v, v2_ref)):
        a_ref[0, :, 0:LANES] = a.astype(BF16)
        a_ref[0, :, LANES:2 * LANES] = pltpu.roll(a, HEAD_DIM, 1).astype(BF16)
    u_ref[0] = z[:, ATTN_WIDTH + 2 * KV_WIDTH:]


def _inproj(x, g, w_in_b, cos, sin):
    b, s, _ = x.shape
    t = TOKEN_TILE
    tok = lambda width: pl.BlockSpec((1, t, width), lambda bi, i: (bi, i, 0))
    table = pl.BlockSpec((t, LANES), lambda bi, i: (i, 0))
    return pl.pallas_call(
        _inproj_kernel,
        grid=(b, s // t),
        in_specs=[tok(D_MODEL), _const_spec((1, D_MODEL)), _const_spec((D_MODEL, IN_PROJ_WIDTH)),
                  table, table],
        out_specs=[tok(ATTN_WIDTH), tok(2 * LANES), tok(2 * LANES), tok(POOL_WIDTH)],
        out_shape=[jax.ShapeDtypeStruct((b, s, ATTN_WIDTH), BF16),
                   jax.ShapeDtypeStruct((b, s, 2 * LANES), BF16),
                   jax.ShapeDtypeStruct((b, s, 2 * LANES), BF16),
                   jax.ShapeDtypeStruct((b, s, POOL_WIDTH), F32)],
        compiler_params=_params(2),
        name="inproj_rope",
    )(x, g, w_in_b, cos, sin)


def _attn_pool_kernel(seq_len, sink_ref, x_ref, q_ref, kp_ref, kc_ref, kn_ref, vp_ref, vc_ref,
                      vn_ref, up_ref, uc_ref, un_ref, wpool_ref, pscale_ref, wout_ref, g_ref,
                      o_ref, kext, vext, uext, cap_ref, s_ref, p_ref, mix_ref):
    t = x_ref.shape[1]
    i = pl.program_id(1)
    n_tiles = pl.num_programs(1)
    n_blocks = t // BLOCK

    kext[0:BLOCK] = kp_ref[0]
    kext[BLOCK:BLOCK + t] = kc_ref[0]
    kext[BLOCK + t:] = kn_ref[0]
    vext[0:BLOCK] = vp_ref[0]
    vext[BLOCK:BLOCK + t] = vc_ref[0]
    vext[BLOCK + t:] = vn_ref[0]

    row = lax.broadcasted_iota(jnp.int32, (BLOCK, BLOCK), 0)
    col = lax.broadcasted_iota(jnp.int32, (BLOCK, BLOCK), 1)
    no_cap = float(jnp.finfo(F32).max)
    cap_ref[1] = jnp.where(col < row, NEG_INF, no_cap)
    cap_ref[2] = jnp.where(col > row, NEG_INF, no_cap)
    cap_ref[0] = jnp.where(col < jnp.maximum(row, jnp.where(i == 0, BLOCK, 0)), NEG_INF, no_cap)
    cap_ref[3] = jnp.where(col > jnp.minimum(row, jnp.where(i == n_tiles - 1, -1, BLOCK)),
                           NEG_INF, no_cap)
    low_half = col < HEAD_DIM
    lane2 = lax.broadcasted_iota(jnp.int32, (2 * BLOCK, LANES), 1)
    half_is = (lane2 < HEAD_DIM, lane2 >= HEAD_DIM)

    for gi, w in enumerate(POOL_WINDOWS):
        half = w // 2
        lanes = slice(gi * POOL_GROUP_DIM, (gi + 1) * POOL_GROUP_DIM)
        uext[gi, 0:POOL_HALO] = jnp.where(i > 0, up_ref[0, :, lanes], 0.0)
        uext[gi, POOL_HALO:POOL_HALO + t] = uc_ref[0, :, lanes]
        uext[gi, POOL_HALO + t:] = jnp.where(i < n_tiles - 1, un_ref[0, :, lanes], 0.0)
        wsum = uext[gi, POOL_HALO - half:POOL_HALO - half + t]
        for d in range(-half + 1, half):
            wsum = wsum + uext[gi, POOL_HALO + d:POOL_HALO + d + t]
        centre = uc_ref[0, :, lanes]

        def edge(r):
            pos = i * t + r + lax.broadcasted_iota(jnp.int32, (POOL_HALO, LANES), 0)
            cnt = jnp.minimum(pos + half, seq_len) - jnp.maximum(pos - half, 0)
            return wsum[r:r + POOL_HALO] / cnt.astype(F32) - centre[r:r + POOL_HALO]

        inner = slice(POOL_HALO, t - POOL_HALO)
        pooled = jnp.concatenate(
            [edge(0), wsum[inner] * (1.0 / w) - centre[inner], edge(t - POOL_HALO)], axis=0)
        mixed = jnp.dot(pooled.astype(BF16), wpool_ref[gi], preferred_element_type=F32)
        mixed = mixed * pscale_ref[:, lanes]
        mix_ref[:, ATTN_WIDTH + gi * POOL_GROUP_DIM:ATTN_WIDTH + (gi + 1) * POOL_GROUP_DIM] = (
            mixed.astype(BF16))

    for jb in range(n_blocks):
        r0 = jb * BLOCK
        slot = jb % 2
        cap_lo = 0 if jb == 0 else 1
        cap_hi = 3 if jb == n_blocks - 1 else 2

        for ge in range(4):
            group, e = ge // 2, ge % 2
            copy = 0 if group == e else 1
            q2 = jnp.concatenate(
                [q_ref[0, r0:r0 + BLOCK, (2 * group + cc) * LANES:(2 * group + cc + 1) * LANES]
                 for cc in range(2)], axis=0)
            q2 = jnp.where(half_is[e], q2, jnp.zeros_like(q2))
            kband = kext[r0:r0 + BAND, copy * LANES:(copy + 1) * LANES]
            s_ref[slot, ge] = lax.dot_general(q2, kband, (((1,), (1,)), ((), ())),
                                              preferred_element_type=F32)

        rinv = {}
        for ge in range(4):
            group, e = ge // 2, ge % 2
            for cc in range(2):
                rows = slice(cc * BLOCK, (cc + 1) * BLOCK)
                sink = sink_ref[2 * (2 * group + cc) + e]
                s0 = jnp.minimum(s_ref[slot, ge, rows, 0:BLOCK], cap_ref[cap_lo])
                s1 = s_ref[slot, ge, rows, BLOCK:2 * BLOCK]
                s2 = jnp.minimum(s_ref[slot, ge, rows, 2 * BLOCK:], cap_ref[cap_hi])
                m = jnp.max(jnp.maximum(jnp.maximum(s0, s1), s2), axis=-1, keepdims=True)
                m = jnp.maximum(m, sink)
                p0 = jnp.exp(s0 - m)
                p1 = jnp.exp(s1 - m)
                p2 = jnp.exp(s2 - m)
                denom = jnp.sum(p0 + p1 + p2, axis=-1, keepdims=True) + jnp.exp(sink - m)
                p_ref[slot, ge, rows, 0:BLOCK] = p0.astype(BF16)
                p_ref[slot, ge, rows, BLOCK:2 * BLOCK] = p1.astype(BF16)
                p_ref[slot, ge, rows, 2 * BLOCK:] = p2.astype(BF16)
                rinv[ge, cc] = 1.0 / denom

        for group in range(2):
            acc = [None, None]
            for e in range(2):
                ge = 2 * group + e
                copy = 0 if group == e else 1
                vband = vext[r0:r0 + BAND, copy * LANES:(copy + 1) * LANES]
                o2 = jnp.dot(p_ref[slot, ge], vband, preferred_element_type=F32)
                for cc in range(2):
                    o = o2[cc * BLOCK:(cc + 1) * BLOCK] * rinv[ge, cc]
                    acc[cc] = o if acc[cc] is None else jnp.where(low_half, acc[cc], o)
            for cc in range(2):
                c = 2 * group + cc
                mix_ref[r0:r0 + BLOCK, c * LANES:(c + 1) * LANES] = acc[cc].astype(BF16)

    for rows in _row_halves(t):
        m = jnp.dot(mix_ref[rows, :], wout_ref[...], preferred_element_type=F32)
        o_ref[0, rows, :] = x_ref[0, rows, :] + _rmsnorm(m, g_ref[...])


def _attn_pool(x, q, k2, v2, u, sink, w_pool_b, pool_scale, w_out_b, g_post):
    b, s, _ = x.shape
    t = TOKEN_TILE
    per_blk = t // BLOCK
    n_blk = s // BLOCK
    per_halo = t // POOL_HALO
    n_halo = s // POOL_HALO
    tok = lambda width: pl.BlockSpec((1, t, width), lambda bi, i: (bi, i, 0))
    blk_prev = pl.BlockSpec((1, BLOCK, 2 * LANES),
                            lambda bi, i: (bi, jnp.maximum(i * per_blk - 1, 0), 0))
    blk_next = pl.BlockSpec((1, BLOCK, 2 * LANES),
                            lambda bi, i: (bi, jnp.minimum((i + 1) * per_blk, n_blk - 1), 0))
    halo_prev = pl.BlockSpec((1, POOL_HALO, POOL_WIDTH),
                             lambda bi, i: (bi, jnp.maximum(i * per_halo - 1, 0), 0))
    halo_next = pl.BlockSpec((1, POOL_HALO, POOL_WIDTH),
                             lambda bi, i: (bi, jnp.minimum((i + 1) * per_halo, n_halo - 1), 0))
    return pl.pallas_call(
        functools.partial(_attn_pool_kernel, s),
        grid=(b, s // t),
        in_specs=[pl.BlockSpec(memory_space=pltpu.SMEM),
                  tok(D_MODEL), tok(ATTN_WIDTH),
                  blk_prev, tok(2 * LANES), blk_next,
                  blk_prev, tok(2 * LANES), blk_next,
                  halo_prev, tok(POOL_WIDTH), halo_next,
                  _const_spec((len(POOL_WINDOWS), POOL_GROUP_DIM, POOL_GROUP_DIM)),
                  _const_spec((1, POOL_WIDTH)),
                  _const_spec((ATTN_WIDTH + POOL_WIDTH, D_MODEL)),
                  _const_spec((1, D_MODEL))],
        out_specs=tok(D_MODEL),
        out_shape=jax.ShapeDtypeStruct((b, s, D_MODEL), F32),
        scratch_shapes=[pltpu.VMEM((t + 2 * BLOCK, 2 * LANES), BF16),
                        pltpu.VMEM((t + 2 * BLOCK, 2 * LANES), BF16),
                        pltpu.VMEM((len(POOL_WINDOWS), t + 2 * POOL_HALO, LANES), F32),
                        pltpu.VMEM((4, BLOCK, BLOCK), F32),
                        pltpu.VMEM((2, 4, 2 * BLOCK, BAND), F32),
                        pltpu.VMEM((2, 4, 2 * BLOCK, BAND), BF16),
                        pltpu.VMEM((t, ATTN_WIDTH + POOL_WIDTH), BF16)],
        compiler_params=_params(2),
        name="attn_pool_outproj",
    )(sink, x, q, k2, k2, k2, v2, v2, v2, u, u, u, w_pool_b, pool_scale, w_out_b, g_post)


def _swiglu_kernel(x_ref, gpre_ref, wg_ref, wu_ref, wd_ref, gpost_ref, o_ref, act_ref):
    x = x_ref[...]
    hb = _rmsnorm(x, gpre_ref[...]).astype(BF16)
    for c in range(D_FF // FF_CHUNK):
        cols = slice(c * FF_CHUNK, (c + 1) * FF_CHUNK)
        gate = jnp.dot(hb, wg_ref[:, cols], preferred_element_type=F32)
        up = jnp.dot(hb, wu_ref[:, cols], preferred_element_type=F32)
        act = (gate * jax.nn.sigmoid(gate)) * up
        act_ref[:, cols] = act.astype(BF16)
    f = jnp.dot(act_ref[...], wd_ref[...], preferred_element_type=F32)
    o_ref[...] = x + _rmsnorm(f, gpost_ref[...])


def _swiglu(x2d, g_pre, wg_b, wu_b, wd_b, g_post, layer):
    n, _ = x2d.shape
    t = TOKEN_TILE
    tok = pl.BlockSpec((t, D_MODEL), lambda i: (i, 0))
    layer_spec = lambda rows, cols: pl.BlockSpec((None, rows, cols), lambda i: (layer, 0, 0),
                                                 pipeline_mode=pl.Buffered(1))
    return pl.pallas_call(
        _swiglu_kernel,
        grid=(n // t,),
        in_specs=[tok, _const_spec((1, D_MODEL)),
                  layer_spec(D_MODEL, D_FF), layer_spec(D_MODEL, D_FF),
                  layer_spec(D_FF, D_MODEL), _const_spec((1, D_MODEL))],
        out_specs=tok,
        out_shape=jax.ShapeDtypeStruct((n, D_MODEL), F32),
        scratch_shapes=[pltpu.VMEM((t, D_FF), BF16)],
        compiler_params=_params(1),
        name="swiglu",
    )(x2d, g_pre, wg_b, wu_b, wd_b, g_post)


def _tree_sum(terms):
    while len(terms) > 1:
        terms = [terms[n] + terms[n + 1] if n + 1 < len(terms) else terms[n]
                 for n in range(0, len(terms), 2)]
    return terms[0]


def _conv_kernel(xp_ref, xc_ref, xn_ref, gpre_ref, w1_ref, b1_ref, wdw_ref, bdw_ref, lng_ref,
                 lnb_ref, w2_ref, b2_ref, gpost_ref, o_ref, gch, gpk, ych):
    t = xc_ref.shape[1]
    i = pl.program_id(1)
    n_tiles = pl.num_programs(1)
    gpre = gpre_ref[...]
    x = xc_ref[0]
    hcat = jnp.concatenate([_rmsnorm(xp_ref[0], gpre).astype(BF16),
                            _rmsnorm(x, gpre).astype(BF16),
                            _rmsnorm(xn_ref[0], gpre).astype(BF16)], axis=0)
    z = jnp.dot(hcat, w1_ref[...], preferred_element_type=F32) + b1_ref[...]
    glu = z[:, :D_MODEL] * jax.nn.sigmoid(z[:, D_MODEL:])
    rowid = lax.broadcasted_iota(jnp.int32, (t + 2 * CONV_HALO, 1), 0)
    first_row = jnp.where(i > 0, 0, CONV_HALO)
    end_row = jnp.where(i < n_tiles - 1, t + 2 * CONV_HALO, t + CONV_HALO)
    glu = jnp.where((rowid >= first_row) & (rowid < end_row), glu, 0.0)
    for c in range(N_CHUNKS):
        gch[c] = glu[:, c * LANES:(c + 1) * LANES]

    n_pairs = gpk.shape[2]

    def chunk_body(c, carry):
        for parity in range(2):
            rows = gch[c, parity:parity + 2 * n_pairs, :].astype(BF16)
            gpk[parity, c] = pltpu.bitcast(rows, jnp.uint32)
        wk = wdw_ref[c].astype(BF16)
        bias = bdw_ref[c]
        for rb in range(t // CONV_ROWS):
            base = rb * CONV_ROWS + CONV_HALO - CONV_WIDTH // 2
            acc = jnp.broadcast_to(bias, (CONV_ROWS, LANES))
            for k0 in range(0, CONV_WIDTH, CONV_TAP_GROUP):
                prods = []
                for k in range(k0, min(k0 + CONV_TAP_GROUP, CONV_WIDTH)):
                    start = base + k
                    window = gpk[start % 2, c, start // 2:start // 2 + CONV_ROWS // 2, :]
                    prods.append(pltpu.bitcast(window, BF16) * wk[k:k + 1, :])
                acc = acc + _tree_sum(prods).astype(F32)
            ych[c, rb * CONV_ROWS:(rb + 1) * CONV_ROWS, :] = acc
        return carry

    lax.fori_loop(0, N_CHUNKS, chunk_body, 0)

    y = jnp.concatenate([ych[c] for c in range(N_CHUNKS)], axis=1)
    mu = jnp.mean(y, axis=-1, keepdims=True)
    var = jnp.mean(jnp.square(y - mu), axis=-1, keepdims=True)
    yn = (y - mu) * lax.rsqrt(var + EPS) * lng_ref[...] + lnb_ref[...]
    act = (yn * jax.nn.sigmoid(yn)).astype(BF16)
    m = jnp.dot(act, w2_ref[...], preferred_element_type=F32) + b2_ref[...]
    o_ref[0] = x + _rmsnorm(m, gpost_ref[...])


def _conv_module(x, g_pre, w1_b, b1, wdw, bdw, ln_g, ln_b, w2_b, b2, g_post):
    b, s, _ = x.shape
    t = TOKEN_TILE
    per_halo = t // CONV_HALO
    n_halo = s // CONV_HALO
    tok = pl.BlockSpec((1, t, D_MODEL), lambda bi, i: (bi, i, 0))
    halo_prev = pl.BlockSpec((1, CONV_HALO, D_MODEL),
                             lambda bi, i: (bi, jnp.maximum(i * per_halo - 1, 0), 0))
    halo_next = pl.BlockSpec((1, CONV_HALO, D_MODEL),
                             lambda bi, i: (bi, jnp.minimum((i + 1) * per_halo, n_halo - 1), 0))
    return pl.pallas_call(
        _conv_kernel,
        grid=(b, s // t),
        in_specs=[halo_prev, tok, halo_next,
                  _const_spec((1, D_MODEL)),
                  _const_spec((D_MODEL, 2 * D_MODEL)), _const_spec((1, 2 * D_MODEL)),
                  _const_spec((N_CHUNKS, CONV_WIDTH, LANES)), _const_spec((N_CHUNKS, 1, LANES)),
                  _const_spec((1, D_MODEL)), _const_spec((1, D_MODEL)),
                  _const_spec((D_MODEL, D_MODEL)), _const_spec((1, D_MODEL)),
                  _const_spec((1, D_MODEL))],
        out_specs=tok,
        out_shape=jax.ShapeDtypeStruct((b, s, D_MODEL), F32),
        scratch_shapes=[pltpu.VMEM((N_CHUNKS, t + 2 * CONV_HALO, LANES), F32),
                        pltpu.VMEM((2, N_CHUNKS, (t + 2 * CONV_HALO) // 2 - 1, LANES), jnp.uint32),
                        pltpu.VMEM((N_CHUNKS, t, LANES), F32)],
        compiler_params=_params(2),
        name="conformer_conv",
    )(x, x, x, g_pre, w1_b, b1, wdw, bdw, ln_g, ln_b, w2_b, b2, g_post)


def _rope_tables(seq_len):
    half = HEAD_DIM // 2
    inv_freq = ROPE_THETA ** (-jnp.arange(0, half, dtype=F32) * 2.0 / HEAD_DIM)
    ang = jnp.arange(seq_len).astype(F32)[:, None] * inv_freq[None, :]
    cos = jnp.cos(ang)
    sin = jnp.sin(ang)
    return jnp.tile(cos, (1, 4)), jnp.concatenate([-sin, sin, -sin, sin], axis=1)


def _chunk_major(w):
    rows = w.shape[0]
    return w.reshape(rows, N_CHUNKS, LANES).transpose(1, 0, 2)


def kernel(x_prompt, x_sample, mix_pre_g, mix_post_g, ffn_pre_g, ffn_post_g, w_in, attn_sink,
           w_pool, pool_scale, w_out, conv_w_pw1, conv_b_pw1, conv_w_dw, conv_b_dw, conv_ln_g,
           conv_ln_b, conv_w_pw2, conv_b_pw2, ffn_w_gate, ffn_w_up, ffn_w_down):
    depth = mix_pre_g.shape[0]
    row = lambda v: v.reshape(1, -1)
    col_scale = jnp.concatenate([jnp.full((ATTN_WIDTH,), HEAD_DIM ** -0.5, F32),
                                 jnp.ones((IN_PROJ_WIDTH - ATTN_WIDTH,), F32)])
    w_in_b = (w_in * col_scale).astype(BF16)
    w_pool_b = w_pool.astype(BF16)
    w_out_b = w_out.astype(BF16)
    w1_b = conv_w_pw1.astype(BF16)
    w2_b = conv_w_pw2.astype(BF16)
    wg_b = ffn_w_gate.astype(BF16)
    wu_b = ffn_w_up.astype(BF16)
    wd_b = ffn_w_down.astype(BF16)

    def trunk(x):
        b, s, _ = x.shape
        cos, sin = _rope_tables(s)
        for layer in range(depth):
            if layer % 2 == 0:
                e = layer // 2
                q, k2, v2, u = _inproj(x, row(mix_pre_g[layer]), w_in_b[e], cos, sin)
                x = _attn_pool(x, q, k2, v2, u, attn_sink[e], w_pool_b[e], row(pool_scale[e]),
                               w_out_b[e], row(mix_post_g[layer]))
            else:
                o = layer // 2
                x = _conv_module(x, row(mix_pre_g[layer]), w1_b[o], row(conv_b_pw1[o]),
                                 _chunk_major(conv_w_dw[o]),
                                 conv_b_dw[o].reshape(N_CHUNKS, 1, LANES),
                                 row(conv_ln_g[o]), row(conv_ln_b[o]), w2_b[o],
                                 row(conv_b_pw2[o]), row(mix_post_g[layer]))
            x = _swiglu(x.reshape(b * s, D_MODEL), row(ffn_pre_g[layer]), wg_b, wu_b, wd_b,
                        row(ffn_post_g[layer]), layer).reshape(b, s, D_MODEL)
        return x

    return (trunk(x_prompt), trunk(x_sample))
```

```python
import functools

import jax
import jax.numpy as jnp
from jax import lax
from jax.experimental import pallas as pl
from jax.experimental.pallas import tpu as pltpu

D_MODEL = 1024
HEAD_DIM = 64
N_Q_HEADS = 8
N_KV_HEADS = 2
ATTN_WIDTH = N_Q_HEADS * HEAD_DIM
KV_WIDTH = N_KV_HEADS * HEAD_DIM
WINDOW = 128
BLOCK = 128
BAND = 3 * BLOCK
ROPE_THETA = 10000.0
POOL_WINDOWS = (2, 4, 8, 16)
POOL_GROUP_DIM = 128
POOL_WIDTH = 512
POOL_HALO = 8
IN_PROJ_WIDTH = ATTN_WIDTH + 2 * KV_WIDTH + POOL_WIDTH
CONV_WIDTH = 31
CONV_HALO = 16
D_FF = 2816
EPS = 1e-6
NEG_INF = -1e30

LANES = 128
N_CHUNKS = D_MODEL // LANES
TOKEN_TILE = 1024
INPROJ_TILE = 2048
FF_CHUNK = 256
CONV_ROWS = 64
CONV_TAP_GROUP = 8
VMEM_LIMIT_BYTES = 56 * 1024 * 1024

F32 = jnp.float32
BF16 = jnp.bfloat16


def _rmsnorm(x, g):
    ms = jnp.mean(x * x, axis=-1, keepdims=True)
    return (x * lax.rsqrt(ms + EPS)) * g


def _params(n_axes):
    return pltpu.CompilerParams(
        dimension_semantics=("parallel",) * n_axes, vmem_limit_bytes=VMEM_LIMIT_BYTES)


def _row_halves(t):
    return (slice(0, t // 2), slice(t // 2, t))


def _const_spec(shape):
    zeros = (0,) * len(shape)
    return pl.BlockSpec(shape, lambda *_: zeros, pipeline_mode=pl.Buffered(1))


def _inproj_kernel(x_ref, g_ref, w_ref, cos_ref, sin_ref, q_ref, k2_ref, v2_ref, u_ref):
    t = x_ref.shape[1]
    h = _rmsnorm(x_ref[0], g_ref[...])
    z = jnp.dot(h.astype(BF16), w_ref[...], preferred_element_type=F32)
    cos = cos_ref[...]
    sin = sin_ref[...]
    lane = lax.broadcasted_iota(jnp.int32, (t, LANES), 1)
    first_half = (lane & (HEAD_DIM - 1)) < HEAD_DIM // 2

    def rope(zc):
        swapped = jnp.where(first_half, pltpu.roll(zc, LANES - HEAD_DIM // 2, 1),
                            pltpu.roll(zc, HEAD_DIM // 2, 1))
        return zc * cos + swapped * sin

    for c in range(ATTN_WIDTH // LANES):
        q_ref[0, :, c * LANES:(c + 1) * LANES] = rope(z[:, c * LANES:(c + 1) * LANES]).astype(BF16)
    k = rope(z[:, ATTN_WIDTH:ATTN_WIDTH + KV_WIDTH])
    v = z[:, ATTN_WIDTH + KV_WIDTH:ATTN_WIDTH + 2 * KV_WIDTH]
    for a, a_ref in ((k, k2_ref), (v, v2_ref)):
        a_ref[0, :, 0:LANES] = a.astype(BF16)
        a_ref[0, :, LANES:2 * LANES] = pltpu.roll(a, HEAD_DIM, 1).astype(BF16)
    u_ref[0] = z[:, ATTN_WIDTH + 2 * KV_WIDTH:]


def _inproj(x, g, w_in_b, cos, sin):
    b, s, _ = x.shape
    t = INPROJ_TILE
    tok = lambda width: pl.BlockSpec((1, t, width), lambda bi, i: (bi, i, 0))
    table = pl.BlockSpec((t, LANES), lambda bi, i: (i, 0))
    return pl.pallas_call(
        _inproj_kernel,
        grid=(b, s // t),
        in_specs=[tok(D_MODEL), _const_spec((1, D_MODEL)), _const_spec((D_MODEL, IN_PROJ_WIDTH)),
                  table, table],
        out_specs=[tok(ATTN_WIDTH), tok(2 * LANES), tok(2 * LANES), tok(POOL_WIDTH)],
        out_shape=[jax.ShapeDtypeStruct((b, s, ATTN_WIDTH), BF16),
                   jax.ShapeDtypeStruct((b, s, 2 * LANES), BF16),
                   jax.ShapeDtypeStruct((b, s, 2 * LANES), BF16),
                   jax.ShapeDtypeStruct((b, s, POOL_WIDTH), F32)],
        compiler_params=_params(2),
        name="inproj_rope",
    )(x, g, w_in_b, cos, sin)


def _attn_pool_kernel(seq_len, sink_ref, x_ref, q_ref, kp_ref, kc_ref, kn_ref, vp_ref, vc_ref,
                      vn_ref, up_ref, uc_ref, un_ref, wpool_ref, pscale_ref, wout_ref, g_ref,
                      o_ref, kext, vext, uext, cap_ref, s_ref, p_ref, mix_ref):
    t = x_ref.shape[1]
    i = pl.program_id(1)
    n_tiles = pl.num_programs(1)
    n_blocks = t // BLOCK

    kext[0:BLOCK] = kp_ref[0]
    kext[BLOCK:BLOCK + t] = kc_ref[0]
    kext[BLOCK + t:] = kn_ref[0]
    vext[0:BLOCK] = vp_ref[0]
    vext[BLOCK:BLOCK + t] = vc_ref[0]
    vext[BLOCK + t:] = vn_ref[0]

    row = lax.broadcasted_iota(jnp.int32, (BLOCK, BLOCK), 0)
    col = lax.broadcasted_iota(jnp.int32, (BLOCK, BLOCK), 1)
    no_cap = float(jnp.finfo(F32).max)
    cap_ref[1] = jnp.where(col < row, NEG_INF, no_cap)
    cap_ref[2] = jnp.where(col > row, NEG_INF, no_cap)
    cap_ref[0] = jnp.where(col < jnp.maximum(row, jnp.where(i == 0, BLOCK, 0)), NEG_INF, no_cap)
    cap_ref[3] = jnp.where(col > jnp.minimum(row, jnp.where(i == n_tiles - 1, -1, BLOCK)),
                           NEG_INF, no_cap)
    low_half = col < HEAD_DIM
    lane2 = lax.broadcasted_iota(jnp.int32, (2 * BLOCK, LANES), 1)
    half_is = (lane2 < HEAD_DIM, lane2 >= HEAD_DIM)

    for gi, w in enumerate(POOL_WINDOWS):
        half = w // 2
        lanes = slice(gi * POOL_GROUP_DIM, (gi + 1) * POOL_GROUP_DIM)
        uext[gi, 0:POOL_HALO] = jnp.where(i > 0, up_ref[0, :, lanes], 0.0)
        uext[gi, POOL_HALO:POOL_HALO + t] = uc_ref[0, :, lanes]
        uext[gi, POOL_HALO + t:] = jnp.where(i < n_tiles - 1, un_ref[0, :, lanes], 0.0)
        wsum = uext[gi, POOL_HALO - half:POOL_HALO - half + t]
        for d in range(-half + 1, half):
            wsum = wsum + uext[gi, POOL_HALO + d:POOL_HALO + d + t]
        centre = uc_ref[0, :, lanes]

        def edge(r):
            pos = i * t + r + lax.broadcasted_iota(jnp.int32, (POOL_HALO, LANES), 0)
            cnt = jnp.minimum(pos + half, seq_len) - jnp.maximum(pos - half, 0)
            return wsum[r:r + POOL_HALO] / cnt.astype(F32) - centre[r:r + POOL_HALO]

        inner = slice(POOL_HALO, t - POOL_HALO)
        pooled = jnp.concatenate(
            [edge(0), wsum[inner] * (1.0 / w) - centre[inner], edge(t - POOL_HALO)], axis=0)
        mixed = jnp.dot(pooled.astype(BF16), wpool_ref[gi], preferred_element_type=F32)
        mixed = mixed * pscale_ref[:, lanes]
        mix_ref[:, ATTN_WIDTH + gi * POOL_GROUP_DIM:ATTN_WIDTH + (gi + 1) * POOL_GROUP_DIM] = (
            mixed.astype(BF16))

    for jb in range(n_blocks):
        r0 = jb * BLOCK
        slot = jb % 2
        cap_lo = 0 if jb == 0 else 1
        cap_hi = 3 if jb == n_blocks - 1 else 2

        for ge in range(4):
            group, e = ge // 2, ge % 2
            copy = 0 if group == e else 1
            q2 = jnp.concatenate(
                [q_ref[0, r0:r0 + BLOCK, (2 * group + cc) * LANES:(2 * group + cc + 1) * LANES]
                 for cc in range(2)], axis=0)
            q2 = jnp.where(half_is[e], q2, jnp.zeros_like(q2))
            kband = kext[r0:r0 + BAND, copy * LANES:(copy + 1) * LANES]
            s_ref[slot, ge] = lax.dot_general(q2, kband, (((1,), (1,)), ((), ())),
                                              preferred_element_type=F32)

        rinv = {}
        for ge in range(4):
            group, e = ge // 2, ge % 2
            for cc in range(2):
                rows = slice(cc * BLOCK, (cc + 1) * BLOCK)
                sink = sink_ref[2 * (2 * group + cc) + e]
                s0 = jnp.minimum(s_ref[slot, ge, rows, 0:BLOCK], cap_ref[cap_lo])
                s1 = s_ref[slot, ge, rows, BLOCK:2 * BLOCK]
                s2 = jnp.minimum(s_ref[slot, ge, rows, 2 * BLOCK:], cap_ref[cap_hi])
                m = jnp.max(jnp.maximum(jnp.maximum(s0, s1), s2), axis=-1, keepdims=True)
                m = jnp.maximum(m, sink)
                p0 = jnp.exp(s0 - m)
                p1 = jnp.exp(s1 - m)
                p2 = jnp.exp(s2 - m)
                denom = jnp.sum(p0 + p1 + p2, axis=-1, keepdims=True) + jnp.exp(sink - m)
                p_ref[slot, ge, rows, 0:BLOCK] = p0.astype(BF16)
                p_ref[slot, ge, rows, BLOCK:2 * BLOCK] = p1.astype(BF16)
                p_ref[slot, ge, rows, 2 * BLOCK:] = p2.astype(BF16)
                rinv[ge, cc] = 1.0 / denom

        for group in range(2):
            acc = [None, None]
            for e in range(2):
                ge = 2 * group + e
                copy = 0 if group == e else 1
                vband = vext[r0:r0 + BAND, copy * LANES:(copy + 1) * LANES]
                o2 = jnp.dot(p_ref[slot, ge], vband, preferred_element_type=F32)
                for cc in range(2):
                    o = o2[cc * BLOCK:(cc + 1) * BLOCK] * rinv[ge, cc]
                    acc[cc] = o if acc[cc] is None else jnp.where(low_half, acc[cc], o)
            for cc in range(2):
                c = 2 * group + cc
                mix_ref[r0:r0 + BLOCK, c * LANES:(c + 1) * LANES] = acc[cc].astype(BF16)

    for rows in _row_halves(t):
        m = jnp.dot(mix_ref[rows, :], wout_ref[...], preferred_element_type=F32)
        o_ref[0, rows, :] = x_ref[0, rows, :] + _rmsnorm(m, g_ref[...])


def _attn_pool(x, q, k2, v2, u, sink, w_pool_b, pool_scale, w_out_b, g_post):
    b, s, _ = x.shape
    t = TOKEN_TILE
    per_blk = t // BLOCK
    n_blk = s // BLOCK
    per_halo = t // POOL_HALO
    n_halo = s // POOL_HALO
    tok = lambda width: pl.BlockSpec((1, t, width), lambda bi, i: (bi, i, 0))
    blk_prev = pl.BlockSpec((1, BLOCK, 2 * LANES),
                            lambda bi, i: (bi, jnp.maximum(i * per_blk - 1, 0), 0))
    blk_next = pl.BlockSpec((1, BLOCK, 2 * LANES),
                            lambda bi, i: (bi, jnp.minimum((i + 1) * per_blk, n_blk - 1), 0))
    halo_prev = pl.BlockSpec((1, POOL_HALO, POOL_WIDTH),
                             lambda bi, i: (bi, jnp.maximum(i * per_halo - 1, 0), 0))
    halo_next = pl.BlockSpec((1, POOL_HALO, POOL_WIDTH),
                             lambda bi, i: (bi, jnp.minimum((i + 1) * per_halo, n_halo - 1), 0))
    return pl.pallas_call(
        functools.partial(_attn_pool_kernel, s),
        grid=(b, s // t),
        in_specs=[pl.BlockSpec(memory_space=pltpu.SMEM),
                  tok(D_MODEL), tok(ATTN_WIDTH),
                  blk_prev, tok(2 * LANES), blk_next,
                  blk_prev, tok(2 * LANES), blk_next,
                  halo_prev, tok(POOL_WIDTH), halo_next,
                  _const_spec((len(POOL_WINDOWS), POOL_GROUP_DIM, POOL_GROUP_DIM)),
                  _const_spec((1, POOL_WIDTH)),
                  _const_spec((ATTN_WIDTH + POOL_WIDTH, D_MODEL)),
                  _const_spec((1, D_MODEL))],
        out_specs=tok(D_MODEL),
        out_shape=jax.ShapeDtypeStruct((b, s, D_MODEL), F32),
        scratch_shapes=[pltpu.VMEM((t + 2 * BLOCK, 2 * LANES), BF16),
                        pltpu.VMEM((t + 2 * BLOCK, 2 * LANES), BF16),
                        pltpu.VMEM((len(POOL_WINDOWS), t + 2 * POOL_HALO, LANES), F32),
                        pltpu.VMEM((4, BLOCK, BLOCK), F32),
                        pltpu.VMEM((2, 4, 2 * BLOCK, BAND), F32),
                        pltpu.VMEM((2, 4, 2 * BLOCK, BAND), BF16),
                        pltpu.VMEM((t, ATTN_WIDTH + POOL_WIDTH), BF16)],
        compiler_params=_params(2),
        name="attn_pool_outproj",
    )(sink, x, q, k2, k2, k2, v2, v2, v2, u, u, u, w_pool_b, pool_scale, w_out_b, g_post)


def _swiglu_kernel(x_ref, gpre_ref, wg_ref, wu_ref, wd_ref, gpost_ref, o_ref, act_ref):
    x = x_ref[...]
    hb = _rmsnorm(x, gpre_ref[...]).astype(BF16)
    for c in range(D_FF // FF_CHUNK):
        cols = slice(c * FF_CHUNK, (c + 1) * FF_CHUNK)
        gate = jnp.dot(hb, wg_ref[:, cols], preferred_element_type=F32)
        up = jnp.dot(hb, wu_ref[:, cols], preferred_element_type=F32)
        act = (gate * jax.nn.sigmoid(gate)) * up
        act_ref[:, cols] = act.astype(BF16)
    f = jnp.dot(act_ref[...], wd_ref[...], preferred_element_type=F32)
    o_ref[...] = x + _rmsnorm(f, gpost_ref[...])


def _swiglu(x2d, g_pre, wg_b, wu_b, wd_b, g_post, layer):
    n, _ = x2d.shape
    t = TOKEN_TILE
    tok = pl.BlockSpec((t, D_MODEL), lambda i: (i, 0))
    layer_spec = lambda rows, cols: pl.BlockSpec((None, rows, cols), lambda i: (layer, 0, 0),
                                                 pipeline_mode=pl.Buffered(1))
    return pl.pallas_call(
        _swiglu_kernel,
        grid=(n // t,),
        in_specs=[tok, _const_spec((1, D_MODEL)),
                  layer_spec(D_MODEL, D_FF), layer_spec(D_MODEL, D_FF),
                  layer_spec(D_FF, D_MODEL), _const_spec((1, D_MODEL))],
        out_specs=tok,
        out_shape=jax.ShapeDtypeStruct((n, D_MODEL), F32),
        scratch_shapes=[pltpu.VMEM((t, D_FF), BF16)],
        compiler_params=_params(1),
        name="swiglu",
    )(x2d, g_pre, wg_b, wu_b, wd_b, g_post)


def _tree_sum(terms):
    while len(terms) > 1:
        terms = [terms[n] + terms[n + 1] if n + 1 < len(terms) else terms[n]
                 for n in range(0, len(terms), 2)]
    return terms[0]


def _conv_kernel(xp_ref, xc_ref, xn_ref, gpre_ref, w1_ref, b1_ref, wdw_ref, bdw_ref, lng_ref,
                 lnb_ref, w2_ref, b2_ref, gpost_ref, o_ref, gch, gpk, ych):
    t = xc_ref.shape[1]
    i = pl.program_id(1)
    n_tiles = pl.num_programs(1)
    gpre = gpre_ref[...]
    x = xc_ref[0]
    hcat = jnp.concatenate([_rmsnorm(xp_ref[0], gpre).astype(BF16),
                            _rmsnorm(x, gpre).astype(BF16),
                            _rmsnorm(xn_ref[0], gpre).astype(BF16)], axis=0)
    z = jnp.dot(hcat, w1_ref[...], preferred_element_type=F32) + b1_ref[...]
    glu = z[:, :D_MODEL] * jax.nn.sigmoid(z[:, D_MODEL:])
    rowid = lax.broadcasted_iota(jnp.int32, (t + 2 * CONV_HALO, 1), 0)
    first_row = jnp.where(i > 0, 0, CONV_HALO)
    end_row = jnp.where(i < n_tiles - 1, t + 2 * CONV_HALO, t + CONV_HALO)
    glu = jnp.where((rowid >= first_row) & (rowid < end_row), glu, 0.0)
    for c in range(N_CHUNKS):
        gch[c] = glu[:, c * LANES:(c + 1) * LANES]

    n_pairs = gpk.shape[2]

    def chunk_body(c, carry):
        for parity in range(2):
            rows = gch[c, parity:parity + 2 * n_pairs, :].astype(BF16)
            gpk[parity, c] = pltpu.bitcast(rows, jnp.uint32)
        wk = wdw_ref[c].astype(BF16)
        bias = bdw_ref[c]
        for rb in range(t // CONV_ROWS):
            base = rb * CONV_ROWS + CONV_HALO - CONV_WIDTH // 2
            acc = jnp.broadcast_to(bias, (CONV_ROWS, LANES))
            for k0 in range(0, CONV_WIDTH, CONV_TAP_GROUP):
                prods = []
                for k in range(k0, min(k0 + CONV_TAP_GROUP, CONV_WIDTH)):
                    start = base + k
                    window = gpk[start % 2, c, start // 2:start // 2 + CONV_ROWS // 2, :]
                    prods.append(pltpu.bitcast(window, BF16) * wk[k:k + 1, :])
                acc = acc + _tree_sum(prods).astype(F32)
            ych[c, rb * CONV_ROWS:(rb + 1) * CONV_ROWS, :] = acc
        return carry

    lax.fori_loop(0, N_CHUNKS, chunk_body, 0)

    y = jnp.concatenate([ych[c] for c in range(N_CHUNKS)], axis=1)
    mu = jnp.mean(y, axis=-1, keepdims=True)
    var = jnp.mean(jnp.square(y - mu), axis=-1, keepdims=True)
    yn = (y - mu) * lax.rsqrt(var + EPS) * lng_ref[...] + lnb_ref[...]
    act = (yn * jax.nn.sigmoid(yn)).astype(BF16)
    m = jnp.dot(act, w2_ref[...], preferred_element_type=F32) + b2_ref[...]
    o_ref[0] = x + _rmsnorm(m, gpost_ref[...])


def _conv_module(x, g_pre, w1_b, b1, wdw, bdw, ln_g, ln_b, w2_b, b2, g_post):
    b, s, _ = x.shape
    t = TOKEN_TILE
    per_halo = t // CONV_HALO
    n_halo = s // CONV_HALO
    tok = pl.BlockSpec((1, t, D_MODEL), lambda bi, i: (bi, i, 0))
    halo_prev = pl.BlockSpec((1, CONV_HALO, D_MODEL),
                             lambda bi, i: (bi, jnp.maximum(i * per_halo - 1, 0), 0))
    halo_next = pl.BlockSpec((1, CONV_HALO, D_MODEL),
                             lambda bi, i: (bi, jnp.minimum((i + 1) * per_halo, n_halo - 1), 0))
    return pl.pallas_call(
        _conv_kernel,
        grid=(b, s // t),
        in_specs=[halo_prev, tok, halo_next,
                  _const_spec((1, D_MODEL)),
                  _const_spec((D_MODEL, 2 * D_MODEL)), _const_spec((1, 2 * D_MODEL)),
                  _const_spec((N_CHUNKS, CONV_WIDTH, LANES)), _const_spec((N_CHUNKS, 1, LANES)),
                  _const_spec((1, D_MODEL)), _const_spec((1, D_MODEL)),
                  _const_spec((D_MODEL, D_MODEL)), _const_spec((1, D_MODEL)),
                  _const_spec((1, D_MODEL))],
        out_specs=tok,
        out_shape=jax.ShapeDtypeStruct((b, s, D_MODEL), F32),
        scratch_shapes=[pltpu.VMEM((N_CHUNKS, t + 2 * CONV_HALO, LANES), F32),
                        pltpu.VMEM((2, N_CHUNKS, (t + 2 * CONV_HALO) // 2 - 1, LANES), jnp.uint32),
                        pltpu.VMEM((N_CHUNKS, t, LANES), F32)],
        compiler_params=_params(2),
        name="conformer_conv",
    )(x, x, x, g_pre, w1_b, b1, wdw, bdw, ln_g, ln_b, w2_b, b2, g_post)


def _rope_tables(seq_len):
    half = HEAD_DIM // 2
    inv_freq = ROPE_THETA ** (-jnp.arange(0, half, dtype=F32) * 2.0 / HEAD_DIM)
    ang = jnp.arange(seq_len).astype(F32)[:, None] * inv_freq[None, :]
    cos = jnp.cos(ang)
    sin = jnp.sin(ang)
    return jnp.tile(cos, (1, 4)), jnp.concatenate([-sin, sin, -sin, sin], axis=1)


def _chunk_major(w):
    rows = w.shape[0]
    return w.reshape(rows, N_CHUNKS, LANES).transpose(1, 0, 2)


def kernel(x_prompt, x_sample, mix_pre_g, mix_post_g, ffn_pre_g, ffn_post_g, w_in, attn_sink,
           w_pool, pool_scale, w_out, conv_w_pw1, conv_b_pw1, conv_w_dw, conv_b_dw, conv_ln_g,
           conv_ln_b, conv_w_pw2, conv_b_pw2, ffn_w_gate, ffn_w_up, ffn_w_down):
    depth = mix_pre_g.shape[0]
    row = lambda v: v.reshape(1, -1)
    col_scale = jnp.concatenate([jnp.full((ATTN_WIDTH,), HEAD_DIM ** -0.5, F32),
                                 jnp.ones((IN_PROJ_WIDTH - ATTN_WIDTH,), F32)])
    w_in_b = (w_in * col_scale).astype(BF16)
    w_pool_b = w_pool.astype(BF16)
    w_out_b = w_out.astype(BF16)
    w1_b = conv_w_pw1.astype(BF16)
    w2_b = conv_w_pw2.astype(BF16)
    wg_b = ffn_w_gate.astype(BF16)
    wu_b = ffn_w_up.astype(BF16)
    wd_b = ffn_w_down.astype(BF16)

    def trunk(x):
        b, s, _ = x.shape
        cos, sin = _rope_tables(s)
        for layer in range(depth):
            if layer % 2 == 0:
                e = layer // 2
                q, k2, v2, u = _inproj(x, row(mix_pre_g[layer]), w_in_b[e], cos, sin)
                x = _attn_pool(x, q, k2, v2, u, attn_sink[e], w_pool_b[e], row(pool_scale[e]),
                               w_out_b[e], row(mix_post_g[layer]))
            else:
                o = layer // 2
                x = _conv_module(x, row(mix_pre_g[layer]), w1_b[o], row(conv_b_pw1[o]),
                                 _chunk_major(conv_w_dw[o]),
                                 conv_b_dw[o].reshape(N_CHUNKS, 1, LANES),
                                 row(conv_ln_g[o]), row(conv_ln_b[o]), w2_b[o],
                                 row(conv_b_pw2[o]), row(mix_post_g[layer]))
            x = _swiglu(x.reshape(b * s, D_MODEL), row(ffn_pre_g[layer]), wg_b, wu_b, wd_b,
                        row(ffn_post_g[layer]), layer).reshape(b, s, D_MODEL)
        return x

    return (trunk(x_prompt), trunk(x_sample))
```

```python
import functools

import jax
import jax.numpy as jnp
from jax import lax
from jax.experimental import pallas as pl
from jax.experimental.pallas import tpu as pltpu

D_MODEL = 1024
HEAD_DIM = 64
N_Q_HEADS = 8
N_KV_HEADS = 2
ATTN_WIDTH = N_Q_HEADS * HEAD_DIM
KV_WIDTH = N_KV_HEADS * HEAD_DIM
WINDOW = 128
BLOCK = 128
BAND = 3 * BLOCK
ROPE_THETA = 10000.0
POOL_WINDOWS = (2, 4, 8, 16)
POOL_GROUP_DIM = 128
POOL_WIDTH = 512
POOL_HALO = 8
IN_PROJ_WIDTH = ATTN_WIDTH + 2 * KV_WIDTH + POOL_WIDTH
CONV_WIDTH = 31
CONV_HALO = 16
D_FF = 2816
EPS = 1e-6
NEG_INF = -1e30

LANES = 128
N_CHUNKS = D_MODEL // LANES
TOKEN_TILE = 1024
INPROJ_TILE = 2048
FF_CHUNK = 256
CONV_ROWS = 64
CONV_TAP_GROUP = 8
VMEM_LIMIT_BYTES = 56 * 1024 * 1024

F32 = jnp.float32
BF16 = jnp.bfloat16


def _rmsnorm(x, g):
    ms = jnp.mean(x * x, axis=-1, keepdims=True)
    return (x * lax.rsqrt(ms + EPS)) * g


def _params(n_axes):
    return pltpu.CompilerParams(
        dimension_semantics=("parallel",) * n_axes, vmem_limit_bytes=VMEM_LIMIT_BYTES)


def _row_halves(t):
    return (slice(0, t // 2), slice(t // 2, t))


def _const_spec(shape):
    zeros = (0,) * len(shape)
    return pl.BlockSpec(shape, lambda *_: zeros, pipeline_mode=pl.Buffered(1))


def _inproj_kernel(x_ref, g_ref, w_ref, cos_ref, sin_ref, q_ref, k2_ref, v2_ref, u_ref):
    t = x_ref.shape[1]
    h = _rmsnorm(x_ref[0], g_ref[...])
    z = jnp.dot(h.astype(BF16), w_ref[...], preferred_element_type=F32)
    cos = cos_ref[...]
    sin = sin_ref[...]
    lane = lax.broadcasted_iota(jnp.int32, (t, LANES), 1)
    first_half = (lane & (HEAD_DIM - 1)) < HEAD_DIM // 2

    def rope(zc):
        swapped = jnp.where(first_half, pltpu.roll(zc, LANES - HEAD_DIM // 2, 1),
                            pltpu.roll(zc, HEAD_DIM // 2, 1))
        return zc * cos + swapped * sin

    for c in range(ATTN_WIDTH // LANES):
        q_ref[0, :, c * LANES:(c + 1) * LANES] = rope(z[:, c * LANES:(c + 1) * LANES]).astype(BF16)
    k = rope(z[:, ATTN_WIDTH:ATTN_WIDTH + KV_WIDTH])
    v = z[:, ATTN_WIDTH + KV_WIDTH:ATTN_WIDTH + 2 * KV_WIDTH]
    k2_ref[0] = k.astype(BF16)
    v2_ref[0] = v.astype(BF16)
    u_ref[0] = z[:, ATTN_WIDTH + 2 * KV_WIDTH:]


def _inproj(x, g, w_in_b, cos, sin):
    b, s, _ = x.shape
    t = INPROJ_TILE
    tok = lambda width: pl.BlockSpec((1, t, width), lambda bi, i: (bi, i, 0))
    table = pl.BlockSpec((t, LANES), lambda bi, i: (i, 0))
    return pl.pallas_call(
        _inproj_kernel,
        grid=(b, s // t),
        in_specs=[tok(D_MODEL), _const_spec((1, D_MODEL)), _const_spec((D_MODEL, IN_PROJ_WIDTH)),
                  table, table],
        out_specs=[tok(ATTN_WIDTH), tok(LANES), tok(LANES), tok(POOL_WIDTH)],
        out_shape=[jax.ShapeDtypeStruct((b, s, ATTN_WIDTH), BF16),
                   jax.ShapeDtypeStruct((b, s, LANES), BF16),
                   jax.ShapeDtypeStruct((b, s, LANES), BF16),
                   jax.ShapeDtypeStruct((b, s, POOL_WIDTH), F32)],
        compiler_params=_params(2),
        name="inproj_rope",
    )(x, g, w_in_b, cos, sin)


def _attn_pool_kernel(seq_len, sink_ref, x_ref, q_ref, kp_ref, kc_ref, kn_ref, vp_ref, vc_ref,
                      vn_ref, up_ref, uc_ref, un_ref, wpool_ref, pscale_ref, wout_ref, g_ref,
                      o_ref, kext, vext, uext, cap_ref, s_ref, p_ref, mix_ref):
    t = x_ref.shape[1]
    i = pl.program_id(1)
    n_tiles = pl.num_programs(1)
    n_blocks = t // BLOCK

    def stage(ext, rows, piece_ref):
        piece = piece_ref[0]
        ext[rows, 0:LANES] = piece
        ext[rows, LANES:2 * LANES] = pltpu.roll(piece.astype(F32), HEAD_DIM, 1).astype(BF16)

    for ext, refs in ((kext, (kp_ref, kc_ref, kn_ref)), (vext, (vp_ref, vc_ref, vn_ref))):
        stage(ext, slice(0, BLOCK), refs[0])
        stage(ext, slice(BLOCK, BLOCK + t), refs[1])
        stage(ext, slice(BLOCK + t, 2 * BLOCK + t), refs[2])

    row = lax.broadcasted_iota(jnp.int32, (BLOCK, BLOCK), 0)
    col = lax.broadcasted_iota(jnp.int32, (BLOCK, BLOCK), 1)
    no_cap = float(jnp.finfo(F32).max)
    cap_ref[1] = jnp.where(col < row, NEG_INF, no_cap)
    cap_ref[2] = jnp.where(col > row, NEG_INF, no_cap)
    cap_ref[0] = jnp.where(col < jnp.maximum(row, jnp.where(i == 0, BLOCK, 0)), NEG_INF, no_cap)
    cap_ref[3] = jnp.where(col > jnp.minimum(row, jnp.where(i == n_tiles - 1, -1, BLOCK)),
                           NEG_INF, no_cap)
    low_half = col < HEAD_DIM
    lane2 = lax.broadcasted_iota(jnp.int32, (2 * BLOCK, LANES), 1)
    half_is = (lane2 < HEAD_DIM, lane2 >= HEAD_DIM)

    for gi, w in enumerate(POOL_WINDOWS):
        half = w // 2
        lanes = slice(gi * POOL_GROUP_DIM, (gi + 1) * POOL_GROUP_DIM)
        uext[gi, 0:POOL_HALO] = jnp.where(i > 0, up_ref[0, :, lanes], 0.0)
        uext[gi, POOL_HALO:POOL_HALO + t] = uc_ref[0, :, lanes]
        uext[gi, POOL_HALO + t:] = jnp.where(i < n_tiles - 1, un_ref[0, :, lanes], 0.0)
        wsum = uext[gi, POOL_HALO - half:POOL_HALO - half + t]
        for d in range(-half + 1, half):
            wsum = wsum + uext[gi, POOL_HALO + d:POOL_HALO + d + t]
        centre = uc_ref[0, :, lanes]

        def edge(r):
            pos = i * t + r + lax.broadcasted_iota(jnp.int32, (POOL_HALO, LANES), 0)
            cnt = jnp.minimum(pos + half, seq_len) - jnp.maximum(pos - half, 0)
            return wsum[r:r + POOL_HALO] / cnt.astype(F32) - centre[r:r + POOL_HALO]

        inner = slice(POOL_HALO, t - POOL_HALO)
        pooled = jnp.concatenate(
            [edge(0), wsum[inner] * (1.0 / w) - centre[inner], edge(t - POOL_HALO)], axis=0)
        mixed = jnp.dot(pooled.astype(BF16), wpool_ref[gi], preferred_element_type=F32)
        mixed = mixed * pscale_ref[:, lanes]
        mix_ref[:, ATTN_WIDTH + gi * POOL_GROUP_DIM:ATTN_WIDTH + (gi + 1) * POOL_GROUP_DIM] = (
            mixed.astype(BF16))

    for jb in range(n_blocks):
        r0 = jb * BLOCK
        slot = jb % 2
        cap_lo = 0 if jb == 0 else 1
        cap_hi = 3 if jb == n_blocks - 1 else 2

        for ge in range(4):
            group, e = ge // 2, ge % 2
            copy = 0 if group == e else 1
            q2 = jnp.concatenate(
                [q_ref[0, r0:r0 + BLOCK, (2 * group + cc) * LANES:(2 * group + cc + 1) * LANES]
                 for cc in range(2)], axis=0)
            q2 = jnp.where(half_is[e], q2, jnp.zeros_like(q2))
            kband = kext[r0:r0 + BAND, copy * LANES:(copy + 1) * LANES]
            s_ref[slot, ge] = lax.dot_general(q2, kband, (((1,), (1,)), ((), ())),
                                              preferred_element_type=F32)

        rinv = {}
        for ge in range(4):
            group, e = ge // 2, ge % 2
            for cc in range(2):
                rows = slice(cc * BLOCK, (cc + 1) * BLOCK)
                sink = sink_ref[2 * (2 * group + cc) + e]
                s0 = jnp.minimum(s_ref[slot, ge, rows, 0:BLOCK], cap_ref[cap_lo])
                s1 = s_ref[slot, ge, rows, BLOCK:2 * BLOCK]
                s2 = jnp.minimum(s_ref[slot, ge, rows, 2 * BLOCK:], cap_ref[cap_hi])
                m = jnp.max(jnp.maximum(jnp.maximum(s0, s1), s2), axis=-1, keepdims=True)
                m = jnp.maximum(m, sink)
                p0 = jnp.exp(s0 - m)
                p1 = jnp.exp(s1 - m)
                p2 = jnp.exp(s2 - m)
                denom = jnp.sum(p0 + p1 + p2, axis=-1, keepdims=True) + jnp.exp(sink - m)
                p_ref[slot, ge, rows, 0:BLOCK] = p0.astype(BF16)
                p_ref[slot, ge, rows, BLOCK:2 * BLOCK] = p1.astype(BF16)
                p_ref[slot, ge, rows, 2 * BLOCK:] = p2.astype(BF16)
                rinv[ge, cc] = 1.0 / denom

        for group in range(2):
            acc = [None, None]
            for e in range(2):
                ge = 2 * group + e
                copy = 0 if group == e else 1
                vband = vext[r0:r0 + BAND, copy * LANES:(copy + 1) * LANES]
                o2 = jnp.dot(p_ref[slot, ge], vband, preferred_element_type=F32)
                for cc in range(2):
                    o = o2[cc * BLOCK:(cc + 1) * BLOCK] * rinv[ge, cc]
                    acc[cc] = o if acc[cc] is None else jnp.where(low_half, acc[cc], o)
            for cc in range(2):
                c = 2 * group + cc
                mix_ref[r0:r0 + BLOCK, c * LANES:(c + 1) * LANES] = acc[cc].astype(BF16)

    for rows in _row_halves(t):
        m = jnp.dot(mix_ref[rows, :], wout_ref[...], preferred_element_type=F32)
        o_ref[0, rows, :] = x_ref[0, rows, :] + _rmsnorm(m, g_ref[...])


def _attn_pool(x, q, k2, v2, u, sink, w_pool_b, pool_scale, w_out_b, g_post):
    b, s, _ = x.shape
    t = TOKEN_TILE
    per_blk = t // BLOCK
    n_blk = s // BLOCK
    per_halo = t // POOL_HALO
    n_halo = s // POOL_HALO
    tok = lambda width: pl.BlockSpec((1, t, width), lambda bi, i: (bi, i, 0))
    blk_prev = pl.BlockSpec((1, BLOCK, LANES),
                            lambda bi, i: (bi, jnp.maximum(i * per_blk - 1, 0), 0))
    blk_next = pl.BlockSpec((1, BLOCK, LANES),
                            lambda bi, i: (bi, jnp.minimum((i + 1) * per_blk, n_blk - 1), 0))
    halo_prev = pl.BlockSpec((1, POOL_HALO, POOL_WIDTH),
                             lambda bi, i: (bi, jnp.maximum(i * per_halo - 1, 0), 0))
    halo_next = pl.BlockSpec((1, POOL_HALO, POOL_WIDTH),
                             lambda bi, i: (bi, jnp.minimum((i + 1) * per_halo, n_halo - 1), 0))
    return pl.pallas_call(
        functools.partial(_attn_pool_kernel, s),
        grid=(b, s // t),
        in_specs=[pl.BlockSpec(memory_space=pltpu.SMEM),
                  tok(D_MODEL), tok(ATTN_WIDTH),
                  blk_prev, tok(LANES), blk_next,
                  blk_prev, tok(LANES), blk_next,
                  halo_prev, tok(POOL_WIDTH), halo_next,
                  _const_spec((len(POOL_WINDOWS), POOL_GROUP_DIM, POOL_GROUP_DIM)),
                  _const_spec((1, POOL_WIDTH)),
                  _const_spec((ATTN_WIDTH + POOL_WIDTH, D_MODEL)),
                  _const_spec((1, D_MODEL))],
        out_specs=tok(D_MODEL),
        out_shape=jax.ShapeDtypeStruct((b, s, D_MODEL), F32),
        scratch_shapes=[pltpu.VMEM((t + 2 * BLOCK, 2 * LANES), BF16),
                        pltpu.VMEM((t + 2 * BLOCK, 2 * LANES), BF16),
                        pltpu.VMEM((len(POOL_WINDOWS), t + 2 * POOL_HALO, LANES), F32),
                        pltpu.VMEM((4, BLOCK, BLOCK), F32),
                        pltpu.VMEM((2, 4, 2 * BLOCK, BAND), F32),
                        pltpu.VMEM((2, 4, 2 * BLOCK, BAND), BF16),
                        pltpu.VMEM((t, ATTN_WIDTH + POOL_WIDTH), BF16)],
        compiler_params=_params(2),
        name="attn_pool_outproj",
    )(sink, x, q, k2, k2, k2, v2, v2, v2, u, u, u, w_pool_b, pool_scale, w_out_b, g_post)


def _swiglu_kernel(x_ref, gpre_ref, wg_ref, wu_ref, wd_ref, gpost_ref, o_ref, act_ref):
    x = x_ref[...]
    hb = _rmsnorm(x, gpre_ref[...]).astype(BF16)
    for c in range(D_FF // FF_CHUNK):
        cols = slice(c * FF_CHUNK, (c + 1) * FF_CHUNK)
        gate = jnp.dot(hb, wg_ref[:, cols], preferred_element_type=F32)
        up = jnp.dot(hb, wu_ref[:, cols], preferred_element_type=F32)
        act = (gate * jax.nn.sigmoid(gate)) * up
        act_ref[:, cols] = act.astype(BF16)
    f = jnp.dot(act_ref[...], wd_ref[...], preferred_element_type=F32)
    o_ref[...] = x + _rmsnorm(f, gpost_ref[...])


def _swiglu(x2d, g_pre, wg_b, wu_b, wd_b, g_post, layer):
    n, _ = x2d.shape
    t = TOKEN_TILE
    tok = pl.BlockSpec((t, D_MODEL), lambda i: (i, 0))
    layer_spec = lambda rows, cols: pl.BlockSpec((None, rows, cols), lambda i: (layer, 0, 0),
                                                 pipeline_mode=pl.Buffered(1))
    return pl.pallas_call(
        _swiglu_kernel,
        grid=(n // t,),
        in_specs=[tok, _const_spec((1, D_MODEL)),
                  layer_spec(D_MODEL, D_FF), layer_spec(D_MODEL, D_FF),
                  layer_spec(D_FF, D_MODEL), _const_spec((1, D_MODEL))],
        out_specs=tok,
        out_shape=jax.ShapeDtypeStruct((n, D_MODEL), F32),
        scratch_shapes=[pltpu.VMEM((t, D_FF), BF16)],
        compiler_params=_params(1),
        name="swiglu",
    )(x2d, g_pre, wg_b, wu_b, wd_b, g_post)


def _tree_sum(terms):
    while len(terms) > 1:
        terms = [terms[n] + terms[n + 1] if n + 1 < len(terms) else terms[n]
                 for n in range(0, len(terms), 2)]
    return terms[0]


def _conv_kernel(xp_ref, xc_ref, xn_ref, gpre_ref, w1_ref, b1_ref, wdw_ref, bdw_ref, lng_ref,
                 lnb_ref, w2_ref, b2_ref, gpost_ref, o_ref, gch, gpk, ych):
    t = xc_ref.shape[1]
    i = pl.program_id(1)
    n_tiles = pl.num_programs(1)
    gpre = gpre_ref[...]
    x = xc_ref[0]
    hcat = jnp.concatenate([_rmsnorm(xp_ref[0], gpre).astype(BF16),
                            _rmsnorm(x, gpre).astype(BF16),
                            _rmsnorm(xn_ref[0], gpre).astype(BF16)], axis=0)
    z = jnp.dot(hcat, w1_ref[...], preferred_element_type=F32) + b1_ref[...]
    glu = z[:, :D_MODEL] * jax.nn.sigmoid(z[:, D_MODEL:])
    rowid = lax.broadcasted_iota(jnp.int32, (t + 2 * CONV_HALO, 1), 0)
    first_row = jnp.where(i > 0, 0, CONV_HALO)
    end_row = jnp.where(i < n_tiles - 1, t + 2 * CONV_HALO, t + CONV_HALO)
    glu = jnp.where((rowid >= first_row) & (rowid < end_row), glu, 0.0)
    for c in range(N_CHUNKS):
        gch[c] = glu[:, c * LANES:(c + 1) * LANES]

    n_pairs = gpk.shape[2]

    def chunk_body(c, carry):
        for parity in range(2):
            rows = gch[c, parity:parity + 2 * n_pairs, :].astype(BF16)
            gpk[parity, c] = pltpu.bitcast(rows, jnp.uint32)
        wk = wdw_ref[c].astype(BF16)
        bias = bdw_ref[c]
        for rb in range(t // CONV_ROWS):
            base = rb * CONV_ROWS + CONV_HALO - CONV_WIDTH // 2
            acc = jnp.broadcast_to(bias, (CONV_ROWS, LANES))
            for k0 in range(0, CONV_WIDTH, CONV_TAP_GROUP):
                prods = []
                for k in range(k0, min(k0 + CONV_TAP_GROUP, CONV_WIDTH)):
                    start = base + k
                    window = gpk[start % 2, c, start // 2:start // 2 + CONV_ROWS // 2, :]
                    prods.append(pltpu.bitcast(window, BF16) * wk[k:k + 1, :])
                acc = acc + _tree_sum(prods).astype(F32)
            ych[c, rb * CONV_ROWS:(rb + 1) * CONV_ROWS, :] = acc
        return carry

    lax.fori_loop(0, N_CHUNKS, chunk_body, 0)

    y = jnp.concatenate([ych[c] for c in range(N_CHUNKS)], axis=1)
    mu = jnp.mean(y, axis=-1, keepdims=True)
    var = jnp.mean(jnp.square(y - mu), axis=-1, keepdims=True)
    yn = (y - mu) * lax.rsqrt(var + EPS) * lng_ref[...] + lnb_ref[...]
    act = (yn * jax.nn.sigmoid(yn)).astype(BF16)
    m = jnp.dot(act, w2_ref[...], preferred_element_type=F32) + b2_ref[...]
    o_ref[0] = x + _rmsnorm(m, gpost_ref[...])


def _conv_module(x, g_pre, w1_b, b1, wdw, bdw, ln_g, ln_b, w2_b, b2, g_post):
    b, s, _ = x.shape
    t = TOKEN_TILE
    per_halo = t // CONV_HALO
    n_halo = s // CONV_HALO
    tok = pl.BlockSpec((1, t, D_MODEL), lambda bi, i: (bi, i, 0))
    halo_prev = pl.BlockSpec((1, CONV_HALO, D_MODEL),
                             lambda bi, i: (bi, jnp.maximum(i * per_halo - 1, 0), 0))
    halo_next = pl.BlockSpec((1, CONV_HALO, D_MODEL),
                             lambda bi, i: (bi, jnp.minimum((i + 1) * per_halo, n_halo - 1), 0))
    return pl.pallas_call(
        _conv_kernel,
        grid=(b, s // t),
        in_specs=[halo_prev, tok, halo_next,
                  _const_spec((1, D_MODEL)),
                  _const_spec((D_MODEL, 2 * D_MODEL)), _const_spec((1, 2 * D_MODEL)),
                  _const_spec((N_CHUNKS, CONV_WIDTH, LANES)), _const_spec((N_CHUNKS, 1, LANES)),
                  _const_spec((1, D_MODEL)), _const_spec((1, D_MODEL)),
                  _const_spec((D_MODEL, D_MODEL)), _const_spec((1, D_MODEL)),
                  _const_spec((1, D_MODEL))],
        out_specs=tok,
        out_shape=jax.ShapeDtypeStruct((b, s, D_MODEL), F32),
        scratch_shapes=[pltpu.VMEM((N_CHUNKS, t + 2 * CONV_HALO, LANES), F32),
                        pltpu.VMEM((2, N_CHUNKS, (t + 2 * CONV_HALO) // 2 - 1, LANES), jnp.uint32),
                        pltpu.VMEM((N_CHUNKS, t, LANES), F32)],
        compiler_params=_params(2),
        name="conformer_conv",
    )(x, x, x, g_pre, w1_b, b1, wdw, bdw, ln_g, ln_b, w2_b, b2, g_post)


def _rope_tables(seq_len):
    half = HEAD_DIM // 2
    inv_freq = ROPE_THETA ** (-jnp.arange(0, half, dtype=F32) * 2.0 / HEAD_DIM)
    ang = jnp.arange(seq_len).astype(F32)[:, None] * inv_freq[None, :]
    cos = jnp.cos(ang)
    sin = jnp.sin(ang)
    return jnp.tile(cos, (1, 4)), jnp.concatenate([-sin, sin, -sin, sin], axis=1)


def _chunk_major(w):
    rows = w.shape[0]
    return w.reshape(rows, N_CHUNKS, LANES).transpose(1, 0, 2)


def kernel(x_prompt, x_sample, mix_pre_g, mix_post_g, ffn_pre_g, ffn_post_g, w_in, attn_sink,
           w_pool, pool_scale, w_out, conv_w_pw1, conv_b_pw1, conv_w_dw, conv_b_dw, conv_ln_g,
           conv_ln_b, conv_w_pw2, conv_b_pw2, ffn_w_gate, ffn_w_up, ffn_w_down):
    depth = mix_pre_g.shape[0]
    row = lambda v: v.reshape(1, -1)
    col_scale = jnp.concatenate([jnp.full((ATTN_WIDTH,), HEAD_DIM ** -0.5, F32),
                                 jnp.ones((IN_PROJ_WIDTH - ATTN_WIDTH,), F32)])
    w_in_b = (w_in * col_scale).astype(BF16)
    w_pool_b = w_pool.astype(BF16)
    w_out_b = w_out.astype(BF16)
    w1_b = conv_w_pw1.astype(BF16)
    w2_b = conv_w_pw2.astype(BF16)
    wg_b = ffn_w_gate.astype(BF16)
    wu_b = ffn_w_up.astype(BF16)
    wd_b = ffn_w_down.astype(BF16)

    def trunk(x):
        b, s, _ = x.shape
        cos, sin = _rope_tables(s)
        for layer in range(depth):
            if layer % 2 == 0:
                e = layer // 2
                q, k2, v2, u = _inproj(x, row(mix_pre_g[layer]), w_in_b[e], cos, sin)
                x = _attn_pool(x, q, k2, v2, u, attn_sink[e], w_pool_b[e], row(pool_scale[e]),
                               w_out_b[e], row(mix_post_g[layer]))
            else:
                o = layer // 2
                x = _conv_module(x, row(mix_pre_g[layer]), w1_b[o], row(conv_b_pw1[o]),
                                 _chunk_major(conv_w_dw[o]),
                                 conv_b_dw[o].reshape(N_CHUNKS, 1, LANES),
                                 row(conv_ln_g[o]), row(conv_ln_b[o]), w2_b[o],
                                 row(conv_b_pw2[o]), row(mix_post_g[layer]))
            x = _swiglu(x.reshape(b * s, D_MODEL), row(ffn_pre_g[layer]), wg_b, wu_b, wd_b,
                        row(ffn_post_g[layer]), layer).reshape(b, s, D_MODEL)
        return x

    return (trunk(x_prompt), trunk(x_sample))
```
